```python
import math
import jax, jax.numpy as jnp
from jax import lax
import numpy as np

D_MODEL = 1024
BATCH = 4
SEQ = 4096
DEPTH = 4

D_MIX = D_MODEL
H_A = 4
DK_A = 32
DV_A = 64
W_A = H_A * DV_A
H_B = 6
DH_B = 64
W_B = H_B * DH_B
DIL_PATTERNS = ((128, 1), (512, 4), (2048, 16))
H_C = 4
DK_C = 48
DV_C = 96
W_C = H_C * DV_C
SPLIT_SIZES = (H_A * 2 * DK_A, H_A * 2 * DK_A, W_A, W_A,
               W_B, W_B, W_B, W_B,
               H_C * DK_C, H_C * DK_C, W_C, W_C)
IN_COLS = 3712
ROT_THETA = 500000.0
ROT_A = DK_A // 4
ROT_B = DH_B // 4
RET_THETA = 10000.0
Q_BLOCK = 128
RET_CHUNK = 128
EPS = 1e-6
NEG = -1e30

kernel_name = "hymba_diff_dilated_retention_encoder"


def rms_norm(x, g):
    xf = x.astype(jnp.float32)
    y = xf * lax.rsqrt(jnp.mean(xf * xf, axis=-1, keepdims=True) + EPS)
    return y.astype(x.dtype) * g.astype(x.dtype)


def rope(x, theta, rot_dim):
    S = x.shape[1]
    half = rot_dim // 2
    pos = jnp.arange(S, dtype=jnp.float32)
    inv = theta ** (-jnp.arange(0, rot_dim, 2, dtype=jnp.float32) / rot_dim)
    ang = pos[:, None] * inv[None, :]
    shape = (S,) + (1,) * (x.ndim - 3) + (half,)
    cos = jnp.cos(ang).reshape(shape).astype(x.dtype)
    sin = jnp.sin(ang).reshape(shape).astype(x.dtype)
    x1 = x[..., :half]
    x2 = x[..., half:rot_dim]
    return jnp.concatenate([x1 * cos - x2 * sin, x1 * sin + x2 * cos, x[..., rot_dim:]], axis=-1)


def diff_attention(q, k, v, lam, lam_init, subln_g):
    B, S = q.shape[:2]
    nb = S // Q_BLOCK
    scale = DK_A ** -0.5
    qb = q.reshape(B, nb, Q_BLOCK, H_A, 2, DK_A).transpose(1, 0, 2, 3, 4, 5)

    def block(qi):
        s = jnp.einsum('bqhmd,bkhmd->bhmqk', qi, k).astype(jnp.float32) * scale
        p = jax.nn.softmax(s, axis=-1)
        pd = p[:, :, 0] - lam * p[:, :, 1]
        return jnp.einsum('bhqk,bkhd->bqhd', pd.astype(v.dtype), v)

    o = lax.map(block, qb)
    o = o.transpose(1, 0, 2, 3, 4).reshape(B, S, H_A, DV_A)
    return rms_norm(o, subln_g) * (1.0 - lam_init)


def dilated_branch(qb, starts, k, v, offsets):
    S = k.shape[1]
    scale = DH_B ** -0.5

    def block(args):
        qi, start = args
        idx = start + jnp.arange(Q_BLOCK)[:, None] + offsets[None, :]
        valid = (idx >= 0) & (idx < S)
        idx_c = jnp.clip(idx, 0, S - 1)
        kg = k[:, idx_c]
        vg = v[:, idx_c]
        s = jnp.einsum('bqhd,bqjhd->bhqj', qi, kg).astype(jnp.float32) * scale
        s = jnp.where(valid[None, None], s, NEG)
        lse = jax.nn.logsumexp(s, axis=-1)
        p = jnp.exp(s - lse[..., None])
        o = jnp.einsum('bhqj,bqjhd->bqhd', p.astype(v.dtype), vg)
        return o, lse

    return lax.map(block, (qb, starts))


def dilated_attention(q, k, v):
    B, S = q.shape[:2]
    nb = S // Q_BLOCK
    qb = q.reshape(B, nb, Q_BLOCK, H_B, DH_B).transpose(1, 0, 2, 3, 4)
    starts = jnp.arange(nb, dtype=jnp.int32) * Q_BLOCK
    outs, lses = [], []
    for window, dil in DIL_PATTERNS:
        n_side = (window // 2) // dil
        offsets = dil * jnp.arange(-n_side, n_side + 1, dtype=jnp.int32)
        o, lse = dilated_branch(qb, starts, k, v, offsets)
        outs.append(o)
        lses.append(lse)
    o = jnp.stack(outs)
    w = jax.nn.softmax(jnp.stack(lses), axis=0)
    w = w.transpose(0, 1, 2, 4, 3)[..., None].astype(o.dtype)
    o = jnp.sum(w * o, axis=0)
    return o.transpose(1, 0, 2, 3, 4).reshape(B, S, H_B, DH_B)


def retention_scan(q, k, v, log_g, include_diag):
    B, H, S, dk = q.shape
    dv = v.shape[-1]
    C = RET_CHUNK
    N = S // C
    i = jnp.arange(C, dtype=jnp.float32)
    diff = i[:, None] - i[None, :]
    mask = (diff >= 0) if include_diag else (diff > 0)
    lg = log_g[:, None, None]
    d_intra = jnp.where(mask[None], jnp.exp(lg * diff[None]), 0.0)
    q_dec = jnp.exp(log_g[:, None] * (i[None, :] + 1.0))[..., None]
    k_dec = jnp.exp(log_g[:, None] * (C - 1.0 - i[None, :]))[..., None]
    chunk_dec = jnp.exp(log_g * C)[:, None, None]

    def to_chunks(t):
        return t.reshape(B, H, N, C, t.shape[-1]).transpose(2, 0, 1, 3, 4)

    def step(R, xs):
        qc, kc, vc = xs
        att = jnp.einsum('bhid,bhjd->bhij', qc, kc) * d_intra
        inner = jnp.einsum('bhij,bhje->bhie', att, vc)
        cross = jnp.einsum('bhid,bhde->bhie', qc, R) * q_dec
        R = R * chunk_dec + jnp.einsum('bhjd,bhje->bhde', kc * k_dec, vc)
        return R, inner + cross

    R0 = jnp.zeros((B, H, dk, dv), jnp.float32)
    _, o = lax.scan(step, R0, (to_chunks(q), to_chunks(k), to_chunks(v)))
    return o.transpose(1, 2, 0, 3, 4).reshape(B, H, S, dv)


def retention_bidir(q, k, v, decay_logit, gn):
    dt = v.dtype
    qf = q.astype(jnp.float32).transpose(0, 2, 1, 3)
    kf = k.astype(jnp.float32).transpose(0, 2, 1, 3) * (DK_C ** -0.5)
    vf = v.astype(jnp.float32).transpose(0, 2, 1, 3)
    log_g = jax.nn.log_sigmoid(decay_logit.astype(jnp.float32))
    fwd = retention_scan(qf, kf, vf, log_g[0], True)
    bwd = jnp.flip(retention_scan(jnp.flip(qf, 2), jnp.flip(kf, 2), jnp.flip(vf, 2), log_g[1], False), 2)
    o = (fwd + bwd).transpose(0, 2, 1, 3)
    return rms_norm(o, gn).astype(dt)


def setup_inputs(seed: int = 0) -> dict:
    key = jax.random.key(seed)
    ks = jax.random.split(key, 20)
    L, D = DEPTH, D_MODEL
    f32 = jnp.float32
    gamma = 1.0 - 2.0 ** (-jnp.linspace(5.0, 12.0, H_C))
    base_logit = jnp.log(gamma) - jnp.log1p(-gamma)
    return {
        "x": jax.random.normal(ks[0], (BATCH, SEQ, D), f32),
        "c": jax.random.normal(ks[1], (BATCH, D), f32),
        "norm_g": 1.0 + 0.02 * jax.random.normal(ks[2], (L, D), f32),
        "w_ada": 0.5 * D ** -0.5 * jax.random.normal(ks[3], (L, D, 3 * D), f32),
        "b_ada": 0.02 * jax.random.normal(ks[4], (L, 3 * D), f32),
        "w_in": D ** -0.5 * jax.random.normal(ks[5], (L, D, IN_COLS), f32),
        "w_out": D_MIX ** -0.5 * jax.random.normal(ks[6], (L, D_MIX, D), f32),
        "qn_a": 1.0 + 0.02 * jax.random.normal(ks[7], (L, DK_A), f32),
        "kn_a": 1.0 + 0.02 * jax.random.normal(ks[8], (L, DK_A), f32),
        "lambda_q1": 0.1 * jax.random.normal(ks[9], (L, DK_A), f32),
        "lambda_k1": 0.1 * jax.random.normal(ks[10], (L, DK_A), f32),
        "lambda_q2": 0.1 * jax.random.normal(ks[11], (L, DK_A), f32),
        "lambda_k2": 0.1 * jax.random.normal(ks[12], (L, DK_A), f32),
        "subln_a": 1.0 + 0.02 * jax.random.normal(ks[13], (L, DV_A), f32),
        "qn_b": 1.0 + 0.02 * jax.random.normal(ks[14], (L, DH_B), f32),
        "kn_b": 1.0 + 0.02 * jax.random.normal(ks[15], (L, DH_B), f32),
        "ret_decay": base_logit[None, None, :] + 0.1 * jax.random.normal(ks[16], (L, 2, H_C), f32),
        "gn_c": 1.0 + 0.02 * jax.random.normal(ks[17], (L, DV_C), f32),
    }


def reference(x, c, norm_g, w_ada, b_ada, w_in, w_out, qn_a, kn_a, lambda_q1, lambda_k1,
              lambda_q2, lambda_k2, subln_a, qn_b, kn_b, ret_decay, gn_c):
    B, S, _ = x.shape
    split_idx = [int(v) for v in np.cumsum(SPLIT_SIZES)[:-1]]
    cs = jax.nn.silu(c)
    for l in range(DEPTH):
        mod = cs @ w_ada[l] + b_ada[l]
        shift, scale, gate = jnp.split(mod, 3, axis=-1)
        h = rms_norm(x, norm_g[l]) * (1.0 + scale[:, None]) + shift[:, None]
        proj = h @ w_in[l]
        qa, ka, va, ga, qb, kb, vb, gb, qc, kc, vc, gc = jnp.split(proj, split_idx, axis=-1)

        qa = rope(rms_norm(qa.reshape(B, S, H_A, 2, DK_A), qn_a[l]), ROT_THETA, ROT_A)
        ka = rope(rms_norm(ka.reshape(B, S, H_A, 2, DK_A), kn_a[l]), ROT_THETA, ROT_A)
        lam_init = 0.8 - 0.6 * math.exp(-0.3 * l)
        lam = (jnp.exp(jnp.sum(lambda_q1[l] * lambda_k1[l]).astype(jnp.float32))
               - jnp.exp(jnp.sum(lambda_q2[l] * lambda_k2[l]).astype(jnp.float32)) + lam_init)
        oa = diff_attention(qa, ka, va.reshape(B, S, H_A, DV_A), lam, lam_init, subln_a[l])
        oa = oa.reshape(B, S, W_A) * jax.nn.silu(ga)

        qb = rope(rms_norm(qb.reshape(B, S, H_B, DH_B), qn_b[l]), ROT_THETA, ROT_B)
        kb = rope(rms_norm(kb.reshape(B, S, H_B, DH_B), kn_b[l]), ROT_THETA, ROT_B)
        ob = dilated_attention(qb, kb, vb.reshape(B, S, H_B, DH_B))
        ob = ob.reshape(B, S, W_B) * jax.nn.silu(gb)

        qc = rope(qc.reshape(B, S, H_C, DK_C), RET_THETA, DK_C)
        kc = rope(kc.reshape(B, S, H_C, DK_C), RET_THETA, DK_C)
        oc = retention_bidir(qc, kc, vc.reshape(B, S, H_C, DV_C), ret_decay[l], gn_c[l])
        oc = oc.reshape(B, S, W_C) * jax.nn.silu(gc)

        y = jnp.concatenate([oa, ob, oc], axis=-1) @ w_out[l]
        x = x + gate[:, None] * y
    return x
```

```python
import functools
import math

import numpy as np
import jax
import jax.numpy as jnp
from jax import lax
from jax.experimental import pallas as pl
from jax.experimental.pallas import tpu as pltpu

F32 = jnp.float32
BF16 = jnp.bfloat16

H_A, DK_A, DV_A = 4, 32, 64
H_B, DH_B = 6, 64
H_C, DK_C, DV_C = 4, 48, 96
SPLIT_SIZES = (256, 256, 256, 256, 384, 384, 384, 384, 192, 192, 384, 384)
ROT_THETA = 500000.0
ROT_A = DK_A // 4
ROT_B = DH_B // 4
RET_THETA = 10000.0
N_SIDE = 64
DILATIONS = (1, 4, 16)
EPS = 1e-6
NEG = -1e30

LANES = 128
V7X_VMEM_BYTES = 64 * 1024 * 1024
VMEM_LIMIT = V7X_VMEM_BYTES * 7 // 8

DKP_C = 64
DVP_C = LANES
W_A = H_A * DV_A
W_B = H_B * DH_B
WQ_C = H_C * DKP_C
WV_C = H_C * DVP_C
C_QA, C_KA, C_VA, C_GA = 0, 256, 512, 768
C_QB, C_KB, C_VB, C_GB = 1024, 1408, 1792, 2176
C_QC, C_KC, C_VC, C_GC = 2560, 2816, 3072, 3584
IN_COLS_P = 4096
OUT_ROWS_P = W_A + W_B + WV_C

RET_CHUNK = 256
Q_TILE_A = 256
Q_BLOCK_B = 128
K_WIN_B = Q_BLOCK_B + 2 * N_SIDE
ROW_TILE = 512

_NT = (((1,), (1,)), ((), ()))


def _cparams(n_grid):
  return pltpu.CompilerParams(
      dimension_semantics=("arbitrary",) * n_grid, vmem_limit_bytes=VMEM_LIMIT)


def _group_ones(group):
  shift = int(math.log2(group))
  r = lax.broadcasted_iota(jnp.int32, (LANES, LANES), 0) >> shift
  c = lax.broadcasted_iota(jnp.int32, (LANES, LANES), 1) >> shift
  return (r == c).astype(BF16)


def _group_sumsq(t, gmat):
  t2 = t * t
  hi = t2.astype(BF16)
  lo = (t2 - hi.astype(F32)).astype(BF16)
  return (jnp.dot(hi, gmat, preferred_element_type=F32)
          + jnp.dot(lo, gmat, preferred_element_type=F32))


def _rope_tile(t, tab_ref, half):
  up = pltpu.roll(t, LANES - half, 1)
  down = pltpu.roll(t, half, 1)
  return t * tab_ref[0] + up * tab_ref[1] + down * tab_ref[2]


def _silu(t):
  return t * jax.nn.sigmoid(t)


def _adaln_kernel(c_ref, w_ref, b_ref, o_ref):
  cs = _silu(c_ref[...])
  o_ref[0] = jnp.dot(cs, w_ref[0], preferred_element_type=F32,
                     precision=lax.Precision.HIGHEST) + b_ref[0]


def _adaln(c_pad, w_ada, b_ada):
  L, D, _ = w_ada.shape
  rows = c_pad.shape[0]
  return pl.pallas_call(
      _adaln_kernel,
      grid=(L, 3),
      in_specs=[
          pl.BlockSpec((rows, D), lambda l, j: (0, 0)),
          pl.BlockSpec((1, D, D), lambda l, j: (l, 0, j)),
          pl.BlockSpec((1, 1, D), lambda l, j: (l, 0, j)),
      ],
      out_specs=pl.BlockSpec((1, rows, D), lambda l, j: (l, 0, j)),
      out_shape=jax.ShapeDtypeStruct((L, rows, 3 * D), F32),
      compiler_params=_cparams(2),
      name="adaln",
  )(c_pad, w_ada, b_ada.reshape(L, 1, 3 * D))


def _inproj_kernel(x_ref, ng_ref, sh_ref, sc_ref, w_ref, gains_ref, taba_ref, tabb_ref, tabc_ref,
                   qa_ref, ka_ref, va_ref, ga_ref, qb_ref, kb_ref, vb_ref, gb_ref,
                   qc_ref, kc_ref, vc_ref, gc_ref, hb_ref):
  x = x_ref[...]
  ms = jnp.mean(x * x, axis=-1, keepdims=True)
  h = x * lax.rsqrt(ms + EPS) * ng_ref[...]
  h = h * (1.0 + sc_ref[...]) + sh_ref[...]
  hb_ref[...] = h.astype(BF16)

  def proj(c0, width):
    return jnp.dot(hb_ref[...], w_ref[:, c0:c0 + width], preferred_element_type=F32)

  def tiles(p):
    return [p[:, LANES * t:LANES * (t + 1)] for t in range(p.shape[1] // LANES)]

  g32 = _group_ones(DK_A)
  g64 = _group_ones(DH_B)
  gains = gains_ref[...]

  def normed(t, gmat, group, gain_row):
    return t * lax.rsqrt(_group_sumsq(t, gmat) * (1.0 / group) + EPS) * gain_row

  for t, p in enumerate(tiles(proj(C_QA, W_A))):
    y = _rope_tile(normed(p, g32, DK_A, gains[0:1]), taba_ref, ROT_A // 2) * (DK_A ** -0.5)
    qa_ref[:, LANES * t:LANES * (t + 1)] = y.astype(BF16)
  for t, p in enumerate(tiles(proj(C_KA, W_A))):
    y = _rope_tile(normed(p, g32, DK_A, gains[1:2]), taba_ref, ROT_A // 2)
    ka_ref[:, LANES * t:LANES * (t + 1)] = y.astype(BF16)
  va_ref[...] = proj(C_VA, W_A).astype(BF16)
  ga_ref[...] = _silu(proj(C_GA, W_A)).astype(BF16)

  for t, p in enumerate(tiles(proj(C_QB, W_B))):
    qb_ref[t] = _rope_tile(normed(p, g64, DH_B, gains[2:3]), tabb_ref, ROT_B // 2) * (DH_B ** -0.5)
  for t, p in enumerate(tiles(proj(C_KB, W_B))):
    kb_ref[t] = _rope_tile(normed(p, g64, DH_B, gains[3:4]), tabb_ref, ROT_B // 2)
  for t, p in enumerate(tiles(proj(C_VB, W_B))):
    vb_ref[t] = p
  gb_ref[...] = _silu(proj(C_GB, W_B)).astype(BF16)

  for t, p in enumerate(tiles(proj(C_QC, WQ_C))):
    qc_ref[:, LANES * t:LANES * (t + 1)] = _rope_tile(p, tabc_ref, DK_C // 2).astype(BF16)
  for t, p in enumerate(tiles(proj(C_KC, WQ_C))):
    y = _rope_tile(p, tabc_ref, DK_C // 2) * (DK_C ** -0.5)
    kc_ref[:, LANES * t:LANES * (t + 1)] = y.astype(BF16)
  vc_ref[...] = proj(C_VC, WV_C).astype(BF16)
  gc_ref[...] = _silu(proj(C_GC, WV_C)).astype(BF16)


def _inproj(x, ng, shift, scale, w, gains, taba, tabb, tabc):
  B, S, D = x.shape
  tm = min(ROW_TILE, S)
  n_pair = W_B // LANES

  def rows(width, dtype=BF16):
    return (pl.BlockSpec((None, tm, width), lambda i, b: (b, i, 0)),
            jax.ShapeDtypeStruct((B, S, width), dtype))

  def slabs():
    return (pl.BlockSpec((None, n_pair, tm, LANES), lambda i, b: (b, 0, i, 0)),
            jax.ShapeDtypeStruct((B, n_pair, S, LANES), F32))

  outs = [rows(W_A), rows(W_A), rows(W_A), rows(W_A),
          slabs(), slabs(), slabs(), rows(W_B),
          rows(WQ_C), rows(WQ_C), rows(WV_C), rows(WV_C)]
  tab_spec = pl.BlockSpec((3, tm, LANES), lambda i, b: (0, i, 0))
  return pl.pallas_call(
      _inproj_kernel,
      grid=(S // tm, B),
      in_specs=[
          pl.BlockSpec((None, tm, D), lambda i, b: (b, i, 0)),
          pl.BlockSpec((1, D), lambda i, b: (0, 0)),
          pl.BlockSpec((None, 1, D), lambda i, b: (b, 0, 0)),
          pl.BlockSpec((None, 1, D), lambda i, b: (b, 0, 0)),
          pl.BlockSpec((D, IN_COLS_P), lambda i, b: (0, 0)),
          pl.BlockSpec((4, LANES), lambda i, b: (0, 0)),
          tab_spec, tab_spec, tab_spec,
      ],
      out_specs=[o[0] for o in outs],
      out_shape=[o[1] for o in outs],
      scratch_shapes=[pltpu.VMEM((tm, D), BF16)],
      compiler_params=_cparams(2),
      name="inproj",
  )(x, ng, shift, scale, w, gains, taba, tabb, tabc)


def _attn_a_kernel(q_ref, k_ref, v_ref, g_ref, lam_ref, sub_ref, o_ref, acc_ref, *, lam_init):
  q = q_ref[...]
  lane = lax.broadcasted_iota(jnp.int32, (1, W_A), 1)
  lp = lam_ref[...]
  lam = (jnp.exp(jnp.sum(lp[0:1] * lp[1:2], axis=-1, keepdims=True))
         - jnp.exp(jnp.sum(lp[2:3] * lp[3:4], axis=-1, keepdims=True)) + lam_init)
  acc_ref[...] = jnp.zeros_like(acc_ref)

  def head(h, carry):
    res = []
    for m in range(2):
      lo = h * DV_A + m * DK_A
      qm = jnp.where((lane >= lo) & (lane < lo + DK_A), q, jnp.zeros_like(q))
      s = lax.dot_general(qm, k_ref[...], _NT, preferred_element_type=F32)
      e = jnp.exp(s - jnp.max(s, axis=-1, keepdims=True))
      l = jnp.sum(e, axis=-1, keepdims=True)
      pv = jnp.dot(e.astype(BF16), v_ref[...], preferred_element_type=F32)
      res.append(pv / l)
    hr = res[0] - lam * res[1]
    sel = (lane >= h * DV_A) & (lane < (h + 1) * DV_A)
    acc_ref[...] = jnp.where(sel, hr, acc_ref[...])
    return carry

  lax.fori_loop(0, H_A, head, 0)

  g64 = _group_ones(DV_A)
  for t in range(W_A // LANES):
    cols = slice(LANES * t, LANES * (t + 1))
    o = acc_ref[:, cols]
    y = o * lax.rsqrt(_group_sumsq(o, g64) * (1.0 / DV_A) + EPS) * sub_ref[:, cols]
    y = y * (1.0 - lam_init) * g_ref[:, cols].astype(F32)
    o_ref[:, cols] = y.astype(BF16)


def _attn_a(qa, ka, va, ga, lam_params, sub, lam_init):
  B, S, _ = qa.shape
  tq = min(Q_TILE_A, S)
  return pl.pallas_call(
      functools.partial(_attn_a_kernel, lam_init=lam_init),
      grid=(B, S // tq),
      in_specs=[
          pl.BlockSpec((None, tq, W_A), lambda b, i: (b, i, 0)),
          pl.BlockSpec((None, S, W_A), lambda b, i: (b, 0, 0)),
          pl.BlockSpec((None, S, W_A), lambda b, i: (b, 0, 0)),
          pl.BlockSpec((None, tq, W_A), lambda b, i: (b, i, 0)),
          pl.BlockSpec((4, DK_A), lambda b, i: (0, 0)),
          pl.BlockSpec((1, W_A), lambda b, i: (0, 0)),
      ],
      out_specs=pl.BlockSpec((None, tq, W_A), lambda b, i: (b, i, 0)),
      out_shape=jax.ShapeDtypeStruct((B, S, W_A), BF16),
      scratch_shapes=[pltpu.VMEM((tq, W_A), F32)],
      compiler_params=_cparams(2),
      name="attn_a",
  )(qa, ka, va, ga, lam_params, sub)


def _attn_b_kernel(q_ref, k_ref, v_ref, g_ref, o_ref, m_ref, l_ref, acc_ref):
  S = q_ref.shape[0]
  lane = lax.broadcasted_iota(jnp.int32, (1, LANES), 1)
  first = lane < DH_B
  ii = lax.broadcasted_iota(jnp.int32, (Q_BLOCK_B, K_WIN_B), 0)
  jj = lax.broadcasted_iota(jnp.int32, (Q_BLOCK_B, K_WIN_B), 1)
  rel = jj - ii

  for gi, d in enumerate(DILATIONS):
    class_len = S // d
    n_qblk = class_len // Q_BLOCK_B

    def item(i, carry, d=d, gi=gi, class_len=class_len, n_qblk=n_qblk):
      r = i // n_qblk
      a0 = (i % n_qblk) * Q_BLOCK_B
      ak = jnp.clip(a0 - N_SIDE, 0, class_len - K_WIN_B)
      if d == 1:
        q_rows = pl.ds(pl.multiple_of(a0, Q_BLOCK_B), Q_BLOCK_B)
        k_rows = pl.ds(pl.multiple_of(ak, N_SIDE), K_WIN_B)
      else:
        q_rows = pl.ds(d * a0 + r, Q_BLOCK_B, stride=d)
        k_rows = pl.ds(d * ak + r, K_WIN_B, stride=d)
      qt = q_ref[q_rows, :]
      kt = k_ref[k_rows, :].astype(BF16)
      vt = v_ref[k_rows, :].astype(BF16)
      valid = jnp.abs(rel + (ak - a0)) <= N_SIDE
      parts = []
      for hh in range(2):
        keep = first if hh == 0 else jnp.logical_not(first)
        qm = jnp.where(keep, qt, 0.0).astype(BF16)
        s = lax.dot_general(qm, kt, _NT, preferred_element_type=F32)
        s = jnp.where(valid, s, NEG)
        m = jnp.max(s, axis=-1, keepdims=True)
        e = jnp.exp(s - m)
        l = jnp.sum(e, axis=-1, keepdims=True)
        pv = jnp.dot(e.astype(BF16), vt, preferred_element_type=F32)
        parts.append((m, l, pv))
      shape = (Q_BLOCK_B, LANES)
      m_new = jnp.where(first, jnp.broadcast_to(parts[0][0], shape), jnp.broadcast_to(parts[1][0], shape))
      l_new = jnp.where(first, jnp.broadcast_to(parts[0][1], shape), jnp.broadcast_to(parts[1][1], shape))
      pv_new = jnp.where(first, parts[0][2], parts[1][2])
      if gi == 0:
        m_ref[q_rows, :] = m_new
        l_ref[q_rows, :] = l_new
        acc_ref[q_rows, :] = pv_new
      else:
        m_old = m_ref[q_rows, :]
        m_tot = jnp.maximum(m_old, m_new)
        w_old = jnp.exp(m_old - m_tot)
        w_new = jnp.exp(m_new - m_tot)
        m_ref[q_rows, :] = m_tot
        l_ref[q_rows, :] = w_old * l_ref[q_rows, :] + w_new * l_new
        acc_ref[q_rows, :] = w_old * acc_ref[q_rows, :] + w_new * pv_new
      return carry

    lax.fori_loop(0, d * n_qblk, item, 0)

  def finish(c, carry):
    rows = pl.ds(pl.multiple_of(c * ROW_TILE, ROW_TILE), ROW_TILE)
    y = acc_ref[rows, :] / l_ref[rows, :] * g_ref[rows, :].astype(F32)
    o_ref[rows, :] = y.astype(BF16)
    return carry

  lax.fori_loop(0, S // ROW_TILE, finish, 0)


def _attn_b(qb, kb, vb, gb):
  B, n_pair, S, _ = qb.shape
  slab = pl.BlockSpec((None, None, S, LANES), lambda b, j: (b, j, 0, 0))
  cols = pl.BlockSpec((None, S, LANES), lambda b, j: (b, 0, j))
  return pl.pallas_call(
      _attn_b_kernel,
      grid=(B, n_pair),
      in_specs=[slab, slab, slab, cols],
      out_specs=cols,
      out_shape=jax.ShapeDtypeStruct((B, S, W_B), BF16),
      scratch_shapes=[pltpu.VMEM((S, LANES), F32)] * 3,
      compiler_params=_cparams(2),
      name="attn_b",
  )(qb, kb, vb, gb)


def _log_sigmoid(t):
  return jnp.minimum(t, 0.0) - jnp.log1p(jnp.exp(-jnp.abs(t)))


def _ret_kernel(q_ref, k_ref, v_ref, g_ref, rdq_ref, rdh_ref, gn_ref, o_ref, u_ref, d_ref, tab_ref):
  S = q_ref.shape[0]
  C = RET_CHUNK
  N = S // C
  lgq = _log_sigmoid(rdq_ref[...])
  lgh = _log_sigmoid(rdh_ref[...])
  pos = lax.broadcasted_iota(jnp.int32, (C, 1), 0).astype(F32)
  tab_ref[0] = jnp.exp(lgq[0:1] * (C - 1.0 - pos))
  tab_ref[1] = jnp.exp(lgq[1:2] * pos)
  tab_ref[2] = jnp.exp(lgq[0:1] * (pos + 1.0))
  tab_ref[3] = jnp.exp(lgq[1:2] * (C - pos))
  dd = (lax.broadcasted_iota(jnp.int32, (C, C), 0) - lax.broadcasted_iota(jnp.int32, (C, C), 1))
  dist = jnp.abs(dd).astype(F32)
  for h in range(H_C):
    d_ref[h] = jnp.exp(jnp.where(dd >= 0, lgh[h:h + 1], lgh[H_C + h:H_C + h + 1]) * dist)
  cdec = jnp.exp(lgh[:, 0:LANES] * float(C))

  lane = lax.broadcasted_iota(jnp.int32, (1, LANES), 1)
  first = lane < DKP_C

  def stack(fwd, bwd, hh):
    if hh == 0:
      return jnp.where(first, fwd, pltpu.roll(bwd, DKP_C, 1))
    return jnp.where(first, pltpu.roll(fwd, DKP_C, 1), bwd)

  def chunk_rows(n):
    return pl.ds(pl.multiple_of(n * C, C), C)

  def local_state(n, carry):
    rows = chunk_rows(n)
    for t in range(WQ_C // LANES):
      cols = slice(LANES * t, LANES * (t + 1))
      kt = k_ref[rows, cols].astype(F32)
      kf = kt * tab_ref[0, :, cols]
      kb = kt * tab_ref[1, :, cols]
      for hh in range(2):
        h = 2 * t + hh
        st = stack(kf, kb, hh).T.astype(BF16)
        u_ref[n, h] = jnp.dot(st, v_ref[rows, DVP_C * h:DVP_C * (h + 1)], preferred_element_type=F32)
    return carry

  lax.fori_loop(0, N, local_state, 0)

  zero = tuple(jnp.zeros((DKP_C, DVP_C), F32) for _ in range(H_C))

  def scan_fwd(n, state):
    new = []
    for h in range(H_C):
      u = u_ref[n, h, 0:DKP_C, :]
      u_ref[n, h, 0:DKP_C, :] = state[h]
      new.append(state[h] * cdec[h:h + 1] + u)
    return tuple(new)

  lax.fori_loop(0, N, scan_fwd, zero)

  def scan_bwd(i, state):
    n = N - 1 - i
    new = []
    for h in range(H_C):
      u = u_ref[n, h, DKP_C:2 * DKP_C, :]
      u_ref[n, h, DKP_C:2 * DKP_C, :] = state[h]
      new.append(state[h] * cdec[H_C + h:H_C + h + 1] + u)
    return tuple(new)

  lax.fori_loop(0, N, scan_bwd, zero)

  def outputs(n, carry):
    rows = chunk_rows(n)
    for t in range(WQ_C // LANES):
      cols = slice(LANES * t, LANES * (t + 1))
      qt = q_ref[rows, cols]
      kt = k_ref[rows, cols]
      q32 = qt.astype(F32)
      qf = q32 * tab_ref[2, :, cols]
      qb = q32 * tab_ref[3, :, cols]
      for hh in range(2):
        h = 2 * t + hh
        vcols = slice(DVP_C * h, DVP_C * (h + 1))
        keep = first if hh == 0 else jnp.logical_not(first)
        qm = jnp.where(keep, qt, jnp.zeros_like(qt))
        att = lax.dot_general(qm, kt, _NT, preferred_element_type=F32) * d_ref[h]
        inner = jnp.dot(att.astype(BF16), v_ref[rows, vcols], preferred_element_type=F32)
        cross = jnp.dot(stack(qf, qb, hh).astype(BF16), u_ref[n, h].astype(BF16),
                        preferred_element_type=F32)
        o = inner + cross
        ms = jnp.sum(o * o, axis=-1, keepdims=True) * (1.0 / DV_C)
        y = o * lax.rsqrt(ms + EPS) * gn_ref[...]
        o_ref[rows, vcols] = (y * g_ref[rows, vcols].astype(F32)).astype(BF16)
    return carry

  lax.fori_loop(0, N, outputs, 0)


def _retention(qc, kc, vc, gc, rdq, rdh, gn):
  B, S, _ = qc.shape
  C = RET_CHUNK
  return pl.pallas_call(
      _ret_kernel,
      grid=(B,),
      in_specs=[
          pl.BlockSpec((None, S, WQ_C), lambda b: (b, 0, 0)),
          pl.BlockSpec((None, S, WQ_C), lambda b: (b, 0, 0)),
          pl.BlockSpec((None, S, WV_C), lambda b: (b, 0, 0)),
          pl.BlockSpec((None, S, WV_C), lambda b: (b, 0, 0)),
          pl.BlockSpec((2, WQ_C), lambda b: (0, 0)),
          pl.BlockSpec((2 * H_C, WQ_C), lambda b: (0, 0)),
          pl.BlockSpec((1, DVP_C), lambda b: (0, 0)),
      ],
      out_specs=pl.BlockSpec((None, S, WV_C), lambda b: (b, 0, 0)),
      out_shape=jax.ShapeDtypeStruct((B, S, WV_C), BF16),
      scratch_shapes=[
          pltpu.VMEM((S // C, H_C, 2 * DKP_C, DVP_C), F32),
          pltpu.VMEM((H_C, C, C), F32),
          pltpu.VMEM((4, C, WQ_C), F32),
      ],
      compiler_params=_cparams(1),
      name="retention",
  )(qc, kc, vc, gc, rdq, rdh, gn)


def _outproj_kernel(oa_ref, ob_ref, oc_ref, w_ref, x_ref, gate_ref, o_ref):
  y = jnp.dot(oa_ref[...], w_ref[0:W_A, :], preferred_element_type=F32)
  y += jnp.dot(ob_ref[...], w_ref[W_A:W_A + W_B, :], preferred_element_type=F32)
  y += jnp.dot(oc_ref[...], w_ref[W_A + W_B:OUT_ROWS_P, :], preferred_element_type=F32)
  o_ref[...] = x_ref[...] + gate_ref[...] * y


def _outproj(oa, ob, oc, w, x, gate):
  B, S, D = x.shape
  tm = min(ROW_TILE, S)

  def rows(width):
    return pl.BlockSpec((None, tm, width), lambda b, i: (b, i, 0))

  return pl.pallas_call(
      _outproj_kernel,
      grid=(B, S // tm),
      in_specs=[rows(W_A), rows(W_B), rows(WV_C),
                pl.BlockSpec((OUT_ROWS_P, D), lambda b, i: (0, 0)),
                rows(D),
                pl.BlockSpec((None, 1, D), lambda b, i: (b, 0, 0))],
      out_specs=rows(D),
      out_shape=jax.ShapeDtypeStruct((B, S, D), F32),
      compiler_params=_cparams(2),
      name="outproj",
  )(oa, ob, oc, w, x, gate)


def _rope_tables(S, theta, rot_dim, group):
  half = rot_dim // 2
  pos = jnp.arange(S, dtype=F32)
  inv = theta ** (-jnp.arange(0, rot_dim, 2, dtype=F32) / rot_dim)
  ang = pos[:, None] * inv[None, :]
  cos, sin = jnp.cos(ang), jnp.sin(ang)
  g = np.arange(LANES) % group
  idx = g % half
  in_first = jnp.asarray(g < half)[None, :]
  in_second = jnp.asarray((g >= half) & (g < rot_dim))[None, :]
  c = jnp.where(in_first | in_second, cos[:, idx], 1.0)
  s1 = jnp.where(in_first, -sin[:, idx], 0.0)
  s2 = jnp.where(in_second, sin[:, idx], 0.0)
  return jnp.stack([c, s1, s2]).astype(F32)


def _pad_heads(w, n_heads, d, d_pad, axis):
  shape = w.shape[:axis] + (n_heads, d) + w.shape[axis + 1:]
  pad = [(0, 0)] * (len(shape))
  pad[axis + 1] = (0, d_pad - d)
  out = jnp.pad(w.reshape(shape), pad)
  return out.reshape(w.shape[:axis] + (n_heads * d_pad,) + w.shape[axis + 1:])


def _prep_w_in(w_in):
  offs = np.concatenate([[0], np.cumsum(SPLIT_SIZES)])
  sec = [w_in[..., offs[i]:offs[i + 1]] for i in range(len(SPLIT_SIZES))]
  sec[8] = _pad_heads(sec[8], H_C, DK_C, DKP_C, w_in.ndim - 1)
  sec[9] = _pad_heads(sec[9], H_C, DK_C, DKP_C, w_in.ndim - 1)
  sec[10] = _pad_heads(sec[10], H_C, DV_C, DVP_C, w_in.ndim - 1)
  sec[11] = _pad_heads(sec[11], H_C, DV_C, DVP_C, w_in.ndim - 1)
  return jnp.concatenate(sec, axis=-1).astype(BF16)


def _prep_w_out(w_out):
  top = w_out[:, :W_A + W_B]
  bot = _pad_heads(w_out[:, W_A + W_B:], H_C, DV_C, DVP_C, 1)
  return jnp.concatenate([top, bot], axis=1).astype(BF16)


@jax.jit
def _forward(x, c, norm_g, w_ada, b_ada, w_in, w_out, qn_a, kn_a, lambda_q1, lambda_k1,
             lambda_q2, lambda_k2, subln_a, qn_b, kn_b, ret_decay, gn_c):
  B, S, D = x.shape
  L = w_in.shape[0]
  rows = 8 * ((B + 7) // 8)
  c_pad = jnp.pad(c, ((0, rows - B), (0, 0)))
  mod = _adaln(c_pad, w_ada, b_ada)[:, :B]
  w_in_p = _prep_w_in(w_in)
  w_out_p = _prep_w_out(w_out)
  taba = _rope_tables(S, ROT_THETA, ROT_A, DK_A)
  tabb = _rope_tables(S, ROT_THETA, ROT_B, DH_B)
  tabc = _rope_tables(S, RET_THETA, DK_C, DKP_C)
  gn_p = jnp.pad(gn_c, ((0, 0), (0, DVP_C - DV_C)))
  for l in range(L):
    shift = mod[l, :, None, 0:D]
    scale = mod[l, :, None, D:2 * D]
    gate = mod[l, :, None, 2 * D:3 * D]
    gains = jnp.stack([jnp.tile(qn_a[l], LANES // DK_A), jnp.tile(kn_a[l], LANES // DK_A),
                       jnp.tile(qn_b[l], LANES // DH_B), jnp.tile(kn_b[l], LANES // DH_B)])
    (qa, ka, va, ga, qb, kb, vb, gb, qc, kc, vc, gc) = _inproj(
        x, norm_g[l][None, :], shift, scale, w_in_p[l], gains, taba, tabb, tabc)
    lam_init = 0.8 - 0.6 * math.exp(-0.3 * l)
    lam_params = jnp.stack([lambda_q1[l], lambda_k1[l], lambda_q2[l], lambda_k2[l]])
    oa = _attn_a(qa, ka, va, ga, lam_params, jnp.tile(subln_a[l], H_A)[None, :], lam_init)
    ob = _attn_b(qb, kb, vb, gb)
    rdq = jnp.repeat(ret_decay[l], DKP_C, axis=1)
    rdh = jnp.broadcast_to(ret_decay[l].reshape(2 * H_C, 1), (2 * H_C, WQ_C))
    oc = _retention(qc, kc, vc, gc, rdq, rdh, gn_p[l][None, :])
    x = _outproj(oa, ob, oc, w_out_p[l], x, gate)
  return x


def kernel(x, c, norm_g, w_ada, b_ada, w_in, w_out, qn_a, kn_a, lambda_q1, lambda_k1, lambda_q2,
           lambda_k2, subln_a, qn_b, kn_b, ret_decay, gn_c):
  return _forward(x, c, norm_g, w_ada, b_ada, w_in, w_out, qn_a, kn_a, lambda_q1, lambda_k1,
                  lambda_q2, lambda_k2, subln_a, qn_b, kn_b, ret_decay, gn_c)
```

```python
import functools
import math

import numpy as np
import jax
import jax.numpy as jnp
from jax import lax
from jax.experimental import pallas as pl
from jax.experimental.pallas import tpu as pltpu

F32 = jnp.float32
BF16 = jnp.bfloat16

H_A, DK_A, DV_A = 4, 32, 64
H_B, DH_B = 6, 64
H_C, DK_C, DV_C = 4, 48, 96
SPLIT_SIZES = (256, 256, 256, 256, 384, 384, 384, 384, 192, 192, 384, 384)
ROT_THETA = 500000.0
ROT_A = DK_A // 4
ROT_B = DH_B // 4
RET_THETA = 10000.0
N_SIDE = 64
DILATIONS = (1, 4, 16)
EPS = 1e-6
NEG = -1e30

LANES = 128
V7X_VMEM_BYTES = 64 * 1024 * 1024
VMEM_LIMIT = V7X_VMEM_BYTES * 7 // 8

DKP_C = 64
DVP_C = LANES
W_A = H_A * DV_A
W_B = H_B * DH_B
WQ_C = H_C * DKP_C
WV_C = H_C * DVP_C
C_QA, C_KA, C_VA, C_GA = 0, 256, 512, 768
C_QB, C_KB, C_VB, C_GB = 1024, 1408, 1792, 2176
C_QC, C_KC, C_VC, C_GC = 2560, 2816, 3072, 3584
IN_COLS_P = 4096
OUT_ROWS_P = W_A + W_B + WV_C

RET_CHUNK = 256
Q_TILE_A = 256
K_CHUNK_A = 1024
LOOKAHEAD_A = 1
VX_ROWS_A = DV_A + 16
LOG2_E = math.log2(math.e)
Q_BLOCK_B = 128
K_WIN_B = Q_BLOCK_B + 2 * N_SIDE
ROW_TILE = 512

_NT = (((1,), (1,)), ((), ()))


def _cparams(n_grid):
  return pltpu.CompilerParams(
      dimension_semantics=("arbitrary",) * n_grid, vmem_limit_bytes=VMEM_LIMIT)


def _group_ones(group):
  shift = int(math.log2(group))
  r = lax.broadcasted_iota(jnp.int32, (LANES, LANES), 0) >> shift
  c = lax.broadcasted_iota(jnp.int32, (LANES, LANES), 1) >> shift
  return (r == c).astype(BF16)


def _group_sumsq(t, gmat):
  t2 = t * t
  hi = t2.astype(BF16)
  lo = (t2 - hi.astype(F32)).astype(BF16)
  return (jnp.dot(hi, gmat, preferred_element_type=F32)
          + jnp.dot(lo, gmat, preferred_element_type=F32))


def _rope_tile(t, tab_ref, half):
  up = pltpu.roll(t, LANES - half, 1)
  down = pltpu.roll(t, half, 1)
  return t * tab_ref[0] + up * tab_ref[1] + down * tab_ref[2]


def _silu(t):
  return t * jax.nn.sigmoid(t)


def _adaln_kernel(c_ref, w_ref, b_ref, o_ref):
  cs = _silu(c_ref[...])
  o_ref[0] = jnp.dot(cs, w_ref[0], preferred_element_type=F32,
                     precision=lax.Precision.HIGHEST) + b_ref[0]


def _adaln(c_pad, w_ada, b_ada):
  L, D, _ = w_ada.shape
  rows = c_pad.shape[0]
  return pl.pallas_call(
      _adaln_kernel,
      grid=(L, 3),
      in_specs=[
          pl.BlockSpec((rows, D), lambda l, j: (0, 0)),
          pl.BlockSpec((1, D, D), lambda l, j: (l, 0, j)),
          pl.BlockSpec((1, 1, D), lambda l, j: (l, 0, j)),
      ],
      out_specs=pl.BlockSpec((1, rows, D), lambda l, j: (l, 0, j)),
      out_shape=jax.ShapeDtypeStruct((L, rows, 3 * D), F32),
      compiler_params=_cparams(2),
      name="adaln",
  )(c_pad, w_ada, b_ada.reshape(L, 1, 3 * D))


def _inproj_kernel(x_ref, ng_ref, sh_ref, sc_ref, w_ref, gains_ref, taba_ref, tabb_ref, tabc_ref,
                   qa_ref, ka_ref, va_ref, ga_ref, qb_ref, kb_ref, vb_ref, gb_ref,
                   qc_ref, kc_ref, vc_ref, gc_ref, hb_ref):
  x = x_ref[...]
  ms = jnp.mean(x * x, axis=-1, keepdims=True)
  h = x * lax.rsqrt(ms + EPS) * ng_ref[...]
  h = h * (1.0 + sc_ref[...]) + sh_ref[...]
  hb_ref[...] = h.astype(BF16)

  def proj(c0, width):
    return jnp.dot(hb_ref[...], w_ref[:, c0:c0 + width], preferred_element_type=F32)

  def tiles(p):
    return [p[:, LANES * t:LANES * (t + 1)] for t in range(p.shape[1] // LANES)]

  g32 = _group_ones(DK_A)
  g64 = _group_ones(DH_B)
  gains = gains_ref[...]

  def normed(t, gmat, group, gain_row):
    return t * lax.rsqrt(_group_sumsq(t, gmat) * (1.0 / group) + EPS) * gain_row

  for t, p in enumerate(tiles(proj(C_QA, W_A))):
    y = _rope_tile(normed(p, g32, DK_A, gains[0:1]), taba_ref, ROT_A // 2) * (DK_A ** -0.5 * LOG2_E)
    qa_ref[LANES * t:LANES * (t + 1), :] = y.T.astype(BF16)
  for t, p in enumerate(tiles(proj(C_KA, W_A))):
    y = _rope_tile(normed(p, g32, DK_A, gains[1:2]), taba_ref, ROT_A // 2)
    ka_ref[:, LANES * t:LANES * (t + 1)] = y.astype(BF16)
  for t, p in enumerate(tiles(proj(C_VA, W_A))):
    va_ref[LANES * t:LANES * (t + 1), :] = p.T.astype(BF16)
  ga_ref[...] = _silu(proj(C_GA, W_A)).astype(BF16)

  for t, p in enumerate(tiles(proj(C_QB, W_B))):
    qb_ref[t] = _rope_tile(normed(p, g64, DH_B, gains[2:3]), tabb_ref, ROT_B // 2) * (DH_B ** -0.5)
  for t, p in enumerate(tiles(proj(C_KB, W_B))):
    kb_ref[t] = _rope_tile(normed(p, g64, DH_B, gains[3:4]), tabb_ref, ROT_B // 2)
  for t, p in enumerate(tiles(proj(C_VB, W_B))):
    vb_ref[t] = p
  gb_ref[...] = _silu(proj(C_GB, W_B)).astype(BF16)

  for t, p in enumerate(tiles(proj(C_QC, WQ_C))):
    qc_ref[:, LANES * t:LANES * (t + 1)] = _rope_tile(p, tabc_ref, DK_C // 2).astype(BF16)
  for t, p in enumerate(tiles(proj(C_KC, WQ_C))):
    y = _rope_tile(p, tabc_ref, DK_C // 2) * (DK_C ** -0.5)
    kc_ref[:, LANES * t:LANES * (t + 1)] = y.astype(BF16)
  vc_ref[...] = proj(C_VC, WV_C).astype(BF16)
  gc_ref[...] = _silu(proj(C_GC, WV_C)).astype(BF16)


def _inproj(x, ng, shift, scale, w, gains, taba, tabb, tabc):
  B, S, D = x.shape
  tm = min(ROW_TILE, S)
  n_pair = W_B // LANES

  def rows(width, dtype=BF16):
    return (pl.BlockSpec((None, tm, width), lambda i, b: (b, i, 0)),
            jax.ShapeDtypeStruct((B, S, width), dtype))

  def slabs():
    return (pl.BlockSpec((None, n_pair, tm, LANES), lambda i, b: (b, 0, i, 0)),
            jax.ShapeDtypeStruct((B, n_pair, S, LANES), F32))

  def feature_major(width):
    return (pl.BlockSpec((None, width, tm), lambda i, b: (b, 0, i)),
            jax.ShapeDtypeStruct((B, width, S), BF16))

  outs = [feature_major(W_A), rows(W_A), feature_major(W_A), rows(W_A),
          slabs(), slabs(), slabs(), rows(W_B),
          rows(WQ_C), rows(WQ_C), rows(WV_C), rows(WV_C)]
  tab_spec = pl.BlockSpec((3, tm, LANES), lambda i, b: (0, i, 0))
  return pl.pallas_call(
      _inproj_kernel,
      grid=(S // tm, B),
      in_specs=[
          pl.BlockSpec((None, tm, D), lambda i, b: (b, i, 0)),
          pl.BlockSpec((1, D), lambda i, b: (0, 0)),
          pl.BlockSpec((None, 1, D), lambda i, b: (b, 0, 0)),
          pl.BlockSpec((None, 1, D), lambda i, b: (b, 0, 0)),
          pl.BlockSpec((D, IN_COLS_P), lambda i, b: (0, 0)),
          pl.BlockSpec((4, LANES), lambda i, b: (0, 0)),
          tab_spec, tab_spec, tab_spec,
      ],
      out_specs=[o[0] for o in outs],
      out_shape=[o[1] for o in outs],
      scratch_shapes=[pltpu.VMEM((tm, D), BF16)],
      compiler_params=_cparams(2),
      name="inproj",
  )(x, ng, shift, scale, w, gains, taba, tabb, tabc)


def _attn_a_kernel(qt_ref, k_ref, vt_ref, g_ref, lam_ref, sub_ref, o_ref,
                   vx_ref, ot_ref, *, lam_init):
  S = k_ref.shape[0]
  nq = qt_ref.shape[1]
  tk = min(K_CHUNK_A, S)

  @pl.when(pl.program_id(1) == 0)
  def _():
    for h in range(H_A):
      vx_ref[h, 0:DV_A, :] = vt_ref[DV_A * h:DV_A * (h + 1), :]
      vx_ref[h, DV_A:VX_ROWS_A, :] = jnp.ones((VX_ROWS_A - DV_A, S), BF16)

  lp = lam_ref[...]
  lam = (jnp.exp(jnp.sum(lp[0:1] * lp[1:2], axis=-1, keepdims=True))
         - jnp.exp(jnp.sum(lp[2:3] * lp[3:4], axis=-1, keepdims=True)) + lam_init)
  row = lax.broadcasted_iota(jnp.int32, (W_A, 1), 0)

  def head(h, carry):
    qt = qt_ref[...]
    qms = []
    for m in range(2):
      lo = h * DV_A + m * DK_A
      qms.append(jnp.where((row >= lo) & (row < lo + DK_A), qt, jnp.zeros_like(qt)))

    def scores(unit):
      m, j = unit
      return jnp.dot(k_ref[j * tk:(j + 1) * tk, :], qms[m], preferred_element_type=F32)

    units = [(m, j) for j in range(S // tk) for m in range(2)]
    pending = [scores(u) for u in units[:LOOKAHEAD_A]]
    run = [None, None]
    for idx, (m, j) in enumerate(units):
      st = pending.pop(0)
      if idx + LOOKAHEAD_A < len(units):
        pending.append(scores(units[idx + LOOKAHEAD_A]))
      cm = jnp.max(st, axis=0, keepdims=True)
      vx = vx_ref[h, :, j * tk:(j + 1) * tk]
      m_new = cm if run[m] is None else jnp.maximum(run[m][0], cm)
      e = jnp.exp2(st - m_new).astype(BF16)
      pv = jnp.dot(vx, e, preferred_element_type=F32)
      if run[m] is None:
        run[m] = (m_new, pv)
      else:
        m_old, acc = run[m]
        run[m] = (m_new, jnp.exp2(m_old - m_new) * acc + pv)
    maps = [acc[0:DV_A] / acc[DV_A:DV_A + 1] for _, acc in run]
    o_h = maps[0] - lam * maps[1]
    ot_ref[pl.ds(pl.multiple_of(h * DV_A, DV_A), DV_A), :] = o_h
    return carry

  lax.fori_loop(0, H_A, head, 0)

  g64 = _group_ones(DV_A)
  for t in range(W_A // LANES):
    cols = slice(LANES * t, LANES * (t + 1))
    o = ot_ref[cols, :].T
    y = o * lax.rsqrt(_group_sumsq(o, g64) * (1.0 / DV_A) + EPS) * sub_ref[:, cols]
    y = y * (1.0 - lam_init) * g_ref[:, cols].astype(F32)
    o_ref[:, cols] = y.astype(BF16)


def _attn_a(qat, ka, vat, ga, lam_params, sub, lam_init):
  B, S, _ = ka.shape
  tq = min(Q_TILE_A, S)
  return pl.pallas_call(
      functools.partial(_attn_a_kernel, lam_init=lam_init),
      grid=(B, S // tq),
      in_specs=[
          pl.BlockSpec((None, W_A, tq), lambda b, i: (b, 0, i)),
          pl.BlockSpec((None, S, W_A), lambda b, i: (b, 0, 0)),
          pl.BlockSpec((None, W_A, S), lambda b, i: (b, 0, 0)),
          pl.BlockSpec((None, tq, W_A), lambda b, i: (b, i, 0)),
          pl.BlockSpec((4, DK_A), lambda b, i: (0, 0)),
          pl.BlockSpec((1, W_A), lambda b, i: (0, 0)),
      ],
      out_specs=pl.BlockSpec((None, tq, W_A), lambda b, i: (b, i, 0)),
      out_shape=jax.ShapeDtypeStruct((B, S, W_A), BF16),
      scratch_shapes=[
          pltpu.VMEM((H_A, VX_ROWS_A, S), BF16),
          pltpu.VMEM((W_A, tq), F32),
      ],
      compiler_params=_cparams(2),
      name="attn_a",
  )(qat, ka, vat, ga, lam_params, sub)


def _attn_b_kernel(q_ref, k_ref, v_ref, g_ref, o_ref, m_ref, l_ref, acc_ref):
  S = q_ref.shape[0]
  lane = lax.broadcasted_iota(jnp.int32, (1, LANES), 1)
  first = lane < DH_B
  ii = lax.broadcasted_iota(jnp.int32, (2 * Q_BLOCK_B, K_WIN_B), 0) & (Q_BLOCK_B - 1)
  jj = lax.broadcasted_iota(jnp.int32, (2 * Q_BLOCK_B, K_WIN_B), 1)
  rel = jj - ii

  for gi, d in enumerate(DILATIONS):
    class_len = S // d
    n_qblk = class_len // Q_BLOCK_B

    def item(i, carry, d=d, gi=gi, class_len=class_len, n_qblk=n_qblk):
      r = i // n_qblk
      a0 = (i % n_qblk) * Q_BLOCK_B
      ak = jnp.clip(a0 - N_SIDE, 0, class_len - K_WIN_B)
      if d == 1:
        q_rows = pl.ds(pl.multiple_of(a0, Q_BLOCK_B), Q_BLOCK_B)
        k_rows = pl.ds(pl.multiple_of(ak, N_SIDE), K_WIN_B)
      else:
        q_rows = pl.ds(d * a0 + r, Q_BLOCK_B, stride=d)
        k_rows = pl.ds(d * ak + r, K_WIN_B, stride=d)
      qt = q_ref[q_rows, :]
      kt = k_ref[k_rows, :].astype(BF16)
      vt = v_ref[k_rows, :].astype(BF16)
      valid = jnp.abs(rel + (ak - a0)) <= N_SIDE
      q2 = jnp.concatenate([jnp.where(first, qt, 0.0), jnp.where(first, 0.0, qt)], axis=0).astype(BF16)
      s = lax.dot_general(q2, kt, _NT, preferred_element_type=F32)
      s = jnp.where(valid, s, NEG)
      m = jnp.max(s, axis=-1, keepdims=True)
      e = jnp.exp(s - m)
      l = jnp.sum(e, axis=-1, keepdims=True)
      pv = jnp.dot(e.astype(BF16), vt, preferred_element_type=F32)
      shape = (Q_BLOCK_B, LANES)
      top, bot = slice(0, Q_BLOCK_B), slice(Q_BLOCK_B, 2 * Q_BLOCK_B)
      m_new = jnp.where(first, jnp.broadcast_to(m[top], shape), jnp.broadcast_to(m[bot], shape))
      l_new = jnp.where(first, jnp.broadcast_to(l[top], shape), jnp.broadcast_to(l[bot], shape))
      pv_new = jnp.where(first, pv[top], pv[bot])
      if gi == 0:
        m_ref[q_rows, :] = m_new
        l_ref[q_rows, :] = l_new
        acc_ref[q_rows, :] = pv_new
      else:
        m_old = m_ref[q_rows, :]
        m_tot = jnp.maximum(m_old, m_new)
        w_old = jnp.exp(m_old - m_tot)
        w_new = jnp.exp(m_new - m_tot)
        m_ref[q_rows, :] = m_tot
        l_ref[q_rows, :] = w_old * l_ref[q_rows, :] + w_new * l_new
        acc_ref[q_rows, :] = w_old * acc_ref[q_rows, :] + w_new * pv_new
      return carry

    lax.fori_loop(0, d * n_qblk, item, 0, unroll=2)

  def finish(c, carry):
    rows = pl.ds(pl.multiple_of(c * ROW_TILE, ROW_TILE), ROW_TILE)
    y = acc_ref[rows, :] / l_ref[rows, :] * g_ref[rows, :].astype(F32)
    o_ref[rows, :] = y.astype(BF16)
    return carry

  lax.fori_loop(0, S // ROW_TILE, finish, 0)


def _attn_b(qb, kb, vb, gb):
  B, n_pair, S, _ = qb.shape
  slab = pl.BlockSpec((None, None, S, LANES), lambda b, j: (b, j, 0, 0))
  cols = pl.BlockSpec((None, S, LANES), lambda b, j: (b, 0, j))
  return pl.pallas_call(
      _attn_b_kernel,
      grid=(B, n_pair),
      in_specs=[slab, slab, slab, cols],
      out_specs=cols,
      out_shape=jax.ShapeDtypeStruct((B, S, W_B), BF16),
      scratch_shapes=[pltpu.VMEM((S, LANES), F32)] * 3,
      compiler_params=_cparams(2),
      name="attn_b",
  )(qb, kb, vb, gb)


def _log_sigmoid(t):
  return jnp.minimum(t, 0.0) - jnp.log1p(jnp.exp(-jnp.abs(t)))


def _ret_kernel(q_ref, k_ref, v_ref, g_ref, rdq_ref, rdh_ref, gn_ref, o_ref, u_ref, d_ref, tab_ref):
  S = q_ref.shape[0]
  C = RET_CHUNK
  N = S // C
  lgq = _log_sigmoid(rdq_ref[...])
  lgh = _log_sigmoid(rdh_ref[...])
  pos = lax.broadcasted_iota(jnp.int32, (C, 1), 0).astype(F32)
  tab_ref[0] = jnp.exp(lgq[0:1] * (C - 1.0 - pos))
  tab_ref[1] = jnp.exp(lgq[1:2] * pos)
  tab_ref[2] = jnp.exp(lgq[0:1] * (pos + 1.0))
  tab_ref[3] = jnp.exp(lgq[1:2] * (C - pos))
  dd = (lax.broadcasted_iota(jnp.int32, (C, C), 0) - lax.broadcasted_iota(jnp.int32, (C, C), 1))
  dist = jnp.abs(dd).astype(F32)
  for h in range(H_C):
    d_ref[h] = jnp.exp(jnp.where(dd >= 0, lgh[h:h + 1], lgh[H_C + h:H_C + h + 1]) * dist)
  cdec = jnp.exp(lgh[:, 0:LANES] * float(C))

  lane = lax.broadcasted_iota(jnp.int32, (1, LANES), 1)
  first = lane < DKP_C

  def stack(fwd, bwd, hh):
    if hh == 0:
      return jnp.where(first, fwd, pltpu.roll(bwd, DKP_C, 1))
    return jnp.where(first, pltpu.roll(fwd, DKP_C, 1), bwd)

  def chunk_rows(n):
    return pl.ds(pl.multiple_of(n * C, C), C)

  def local_state(n, carry):
    rows = chunk_rows(n)
    for t in range(WQ_C // LANES):
      cols = slice(LANES * t, LANES * (t + 1))
      kt = k_ref[rows, cols].astype(F32)
      kf = kt * tab_ref[0, :, cols]
      kb = kt * tab_ref[1, :, cols]
      for hh in range(2):
        h = 2 * t + hh
        st = stack(kf, kb, hh).T.astype(BF16)
        u_ref[n, h] = jnp.dot(st, v_ref[rows, DVP_C * h:DVP_C * (h + 1)], preferred_element_type=F32)
    return carry

  lax.fori_loop(0, N, local_state, 0)

  zero = tuple(jnp.zeros((DKP_C, DVP_C), F32) for _ in range(H_C))

  def scan_fwd(n, state):
    new = []
    for h in range(H_C):
      u = u_ref[n, h, 0:DKP_C, :]
      u_ref[n, h, 0:DKP_C, :] = state[h]
      new.append(state[h] * cdec[h:h + 1] + u)
    return tuple(new)

  lax.fori_loop(0, N, scan_fwd, zero)

  def scan_bwd(i, state):
    n = N - 1 - i
    new = []
    for h in range(H_C):
      u = u_ref[n, h, DKP_C:2 * DKP_C, :]
      u_ref[n, h, DKP_C:2 * DKP_C, :] = state[h]
      new.append(state[h] * cdec[H_C + h:H_C + h + 1] + u)
    return tuple(new)

  lax.fori_loop(0, N, scan_bwd, zero)

  def outputs(n, carry):
    rows = chunk_rows(n)
    for t in range(WQ_C // LANES):
      cols = slice(LANES * t, LANES * (t + 1))
      qt = q_ref[rows, cols]
      kt = k_ref[rows, cols]
      q32 = qt.astype(F32)
      qf = q32 * tab_ref[2, :, cols]
      qb = q32 * tab_ref[3, :, cols]
      for hh in range(2):
        h = 2 * t + hh
        vcols = slice(DVP_C * h, DVP_C * (h + 1))
        keep = first if hh == 0 else jnp.logical_not(first)
        qm = jnp.where(keep, qt, jnp.zeros_like(qt))
        att = lax.dot_general(qm, kt, _NT, preferred_element_type=F32) * d_ref[h]
        inner = jnp.dot(att.astype(BF16), v_ref[rows, vcols], preferred_element_type=F32)
        cross = jnp.dot(stack(qf, qb, hh).astype(BF16), u_ref[n, h].astype(BF16),
                        preferred_element_type=F32)
        o = inner + cross
        ms = jnp.sum(o * o, axis=-1, keepdims=True) * (1.0 / DV_C)
        y = o * lax.rsqrt(ms + EPS) * gn_ref[...]
        o_ref[rows, vcols] = (y * g_ref[rows, vcols].astype(F32)).astype(BF16)
    return carry

  lax.fori_loop(0, N, outputs, 0)


def _retention(qc, kc, vc, gc, rdq, rdh, gn):
  B, S, _ = qc.shape
  C = RET_CHUNK
  return pl.pallas_call(
      _ret_kernel,
      grid=(B,),
      in_specs=[
          pl.BlockSpec((None, S, WQ_C), lambda b: (b, 0, 0)),
          pl.BlockSpec((None, S, WQ_C), lambda b: (b, 0, 0)),
          pl.BlockSpec((None, S, WV_C), lambda b: (b, 0, 0)),
          pl.BlockSpec((None, S, WV_C), lambda b: (b, 0, 0)),
          pl.BlockSpec((2, WQ_C), lambda b: (0, 0)),
          pl.BlockSpec((2 * H_C, WQ_C), lambda b: (0, 0)),
          pl.BlockSpec((1, DVP_C), lambda b: (0, 0)),
      ],
      out_specs=pl.BlockSpec((None, S, WV_C), lambda b: (b, 0, 0)),
      out_shape=jax.ShapeDtypeStruct((B, S, WV_C), BF16),
      scratch_shapes=[
          pltpu.VMEM((S // C, H_C, 2 * DKP_C, DVP_C), F32),
          pltpu.VMEM((H_C, C, C), F32),
          pltpu.VMEM((4, C, WQ_C), F32),
      ],
      compiler_params=_cparams(1),
      name="retention",
  )(qc, kc, vc, gc, rdq, rdh, gn)


def _outproj_kernel(oa_ref, ob_ref, oc_ref, w_ref, x_ref, gate_ref, o_ref):
  y = jnp.dot(oa_ref[...], w_ref[0:W_A, :], preferred_element_type=F32)
  y += jnp.dot(ob_ref[...], w_ref[W_A:W_A + W_B, :], preferred_element_type=F32)
  y += jnp.dot(oc_ref[...], w_ref[W_A + W_B:OUT_ROWS_P, :], preferred_element_type=F32)
  o_ref[...] = x_ref[...] + gate_ref[...] * y


def _outproj(oa, ob, oc, w, x, gate):
  B, S, D = x.shape
  tm = min(ROW_TILE, S)

  def rows(width):
    return pl.BlockSpec((None, tm, width), lambda b, i: (b, i, 0))

  return pl.pallas_call(
      _outproj_kernel,
      grid=(B, S // tm),
      in_specs=[rows(W_A), rows(W_B), rows(WV_C),
                pl.BlockSpec((OUT_ROWS_P, D), lambda b, i: (0, 0)),
                rows(D),
                pl.BlockSpec((None, 1, D), lambda b, i: (b, 0, 0))],
      out_specs=rows(D),
      out_shape=jax.ShapeDtypeStruct((B, S, D), F32),
      compiler_params=_cparams(2),
      name="outproj",
  )(oa, ob, oc, w, x, gate)


def _rope_tables(S, theta, rot_dim, group):
  half = rot_dim // 2
  pos = jnp.arange(S, dtype=F32)
  inv = theta ** (-jnp.arange(0, rot_dim, 2, dtype=F32) / rot_dim)
  ang = pos[:, None] * inv[None, :]
  cos, sin = jnp.cos(ang), jnp.sin(ang)
  g = np.arange(LANES) % group
  idx = g % half
  in_first = jnp.asarray(g < half)[None, :]
  in_second = jnp.asarray((g >= half) & (g < rot_dim))[None, :]
  c = jnp.where(in_first | in_second, cos[:, idx], 1.0)
  s1 = jnp.where(in_first, -sin[:, idx], 0.0)
  s2 = jnp.where(in_second, sin[:, idx], 0.0)
  return jnp.stack([c, s1, s2]).astype(F32)


def _pad_heads(w, n_heads, d, d_pad, axis):
  shape = w.shape[:axis] + (n_heads, d) + w.shape[axis + 1:]
  pad = [(0, 0)] * (len(shape))
  pad[axis + 1] = (0, d_pad - d)
  out = jnp.pad(w.reshape(shape), pad)
  return out.reshape(w.shape[:axis] + (n_heads * d_pad,) + w.shape[axis + 1:])


def _prep_w_in(w_in):
  offs = np.concatenate([[0], np.cumsum(SPLIT_SIZES)])
  sec = [w_in[..., offs[i]:offs[i + 1]] for i in range(len(SPLIT_SIZES))]
  sec[8] = _pad_heads(sec[8], H_C, DK_C, DKP_C, w_in.ndim - 1)
  sec[9] = _pad_heads(sec[9], H_C, DK_C, DKP_C, w_in.ndim - 1)
  sec[10] = _pad_heads(sec[10], H_C, DV_C, DVP_C, w_in.ndim - 1)
  sec[11] = _pad_heads(sec[11], H_C, DV_C, DVP_C, w_in.ndim - 1)
  return jnp.concatenate(sec, axis=-1).astype(BF16)


def _prep_w_out(w_out):
  top = w_out[:, :W_A + W_B]
  bot = _pad_heads(w_out[:, W_A + W_B:], H_C, DV_C, DVP_C, 1)
  return jnp.concatenate([top, bot], axis=1).astype(BF16)


@jax.jit
def _forward(x, c, norm_g, w_ada, b_ada, w_in, w_out, qn_a, kn_a, lambda_q1, lambda_k1,
             lambda_q2, lambda_k2, subln_a, qn_b, kn_b, ret_decay, gn_c):
  B, S, D = x.shape
  L = w_in.shape[0]
  rows = 8 * ((B + 7) // 8)
  c_pad = jnp.pad(c, ((0, rows - B), (0, 0)))
  mod = _adaln(c_pad, w_ada, b_ada)[:, :B]
  w_in_p = _prep_w_in(w_in)
  w_out_p = _prep_w_out(w_out)
  taba = _rope_tables(S, ROT_THETA, ROT_A, DK_A)
  tabb = _rope_tables(S, ROT_THETA, ROT_B, DH_B)
  tabc = _rope_tables(S, RET_THETA, DK_C, DKP_C)
  gn_p = jnp.pad(gn_c, ((0, 0), (0, DVP_C - DV_C)))
  for l in range(L):
    shift = mod[l, :, None, 0:D]
    scale = mod[l, :, None, D:2 * D]
    gate = mod[l, :, None, 2 * D:3 * D]
    gains = jnp.stack([jnp.tile(qn_a[l], LANES // DK_A), jnp.tile(kn_a[l], LANES // DK_A),
                       jnp.tile(qn_b[l], LANES // DH_B), jnp.tile(kn_b[l], LANES // DH_B)])
    (qa, ka, va, ga, qb, kb, vb, gb, qc, kc, vc, gc) = _inproj(
        x, norm_g[l][None, :], shift, scale, w_in_p[l], gains, taba, tabb, tabc)
    lam_init = 0.8 - 0.6 * math.exp(-0.3 * l)
    lam_params = jnp.stack([lambda_q1[l], lambda_k1[l], lambda_q2[l], lambda_k2[l]])
    oa = _attn_a(qa, ka, va, ga, lam_params, jnp.tile(subln_a[l], H_A)[None, :], lam_init)
    ob = _attn_b(qb, kb, vb, gb)
    rdq = jnp.repeat(ret_decay[l], DKP_C, axis=1)
    rdh = jnp.broadcast_to(ret_decay[l].reshape(2 * H_C, 1), (2 * H_C, WQ_C))
    oc = _retention(qc, kc, vc, gc, rdq, rdh, gn_p[l][None, :])
    x = _outproj(oa, ob, oc, w_out_p[l], x, gate)
  return x


def kernel(x, c, norm_g, w_ada, b_ada, w_in, w_out, qn_a, kn_a, lambda_q1, lambda_k1, lambda_q2,
           lambda_k2, subln_a, qn_b, kn_b, ret_decay, gn_c):
  return _forward(x, c, norm_g, w_ada, b_ada, w_in, w_out, qn_a, kn_a, lambda_q1, lambda_k1,
                  lambda_q2, lambda_k2, subln_a, qn_b, kn_b, ret_decay, gn_c)
```

```python
import functools
import math

import numpy as np
import jax
import jax.numpy as jnp
from jax import lax
from jax.experimental import pallas as pl
from jax.experimental.pallas import tpu as pltpu

F32 = jnp.float32
BF16 = jnp.bfloat16

H_A, DK_A, DV_A = 4, 32, 64
H_B, DH_B = 6, 64
H_C, DK_C, DV_C = 4, 48, 96
SPLIT_SIZES = (256, 256, 256, 256, 384, 384, 384, 384, 192, 192, 384, 384)
ROT_THETA = 500000.0
ROT_A = DK_A // 4
ROT_B = DH_B // 4
RET_THETA = 10000.0
N_SIDE = 64
DILATIONS = (1, 4, 16)
EPS = 1e-6
NEG = -1e30

LANES = 128
MXU_DIM = 256
V7X_VMEM_BYTES = 64 * 1024 * 1024
VMEM_LIMIT = V7X_VMEM_BYTES * 7 // 8

DKP_C = 64
DVP_C = LANES
W_A = H_A * DV_A
W_B = H_B * DH_B
WQ_C = H_C * DKP_C
WV_C = H_C * DVP_C
C_QA, C_KA, C_VA, C_GA = 0, 256, 512, 768
C_QB, C_KB, C_VB, C_GB = 1024, 1408, 1792, 2176
C_QC, C_KC, C_VC, C_GC = 2560, 2816, 3072, 3584
IN_COLS_P = 4096
OUT_ROWS_P = W_A + W_B + WV_C

RET_CHUNK = 256
Q_TILE_A = 256
K_CHUNK_A = 256
LOOKAHEAD_A = 6
VX_ROWS_A = DV_A + 16
LOG2_E = math.log2(math.e)
Q_BLOCK_B = 128
K_WIN_B = Q_BLOCK_B + 2 * N_SIDE
ITEMS_B = 4
ROW_TILE = 512

_NT = (((1,), (1,)), ((), ()))


def _cparams(n_grid):
  return pltpu.CompilerParams(
      dimension_semantics=("arbitrary",) * n_grid, vmem_limit_bytes=VMEM_LIMIT)


def _group_ones(group, width=LANES):
  shift = int(math.log2(group))
  r = lax.broadcasted_iota(jnp.int32, (width, width), 0) >> shift
  c = lax.broadcasted_iota(jnp.int32, (width, width), 1) >> shift
  return (r == c).astype(BF16)


def _group_sumsq(t, group):
  t2 = (t * t).astype(BF16)
  parts = []
  for c0 in range(0, t.shape[1], MXU_DIM):
    w = min(MXU_DIM, t.shape[1] - c0)
    parts.append(jnp.dot(t2[:, c0:c0 + w], _group_ones(group, w), preferred_element_type=F32))
  return parts[0] if len(parts) == 1 else jnp.concatenate(parts, axis=1)


def _rope_tile(t, tab_ref, half):
  up = pltpu.roll(t, LANES - half, 1)
  down = pltpu.roll(t, half, 1)
  return t * tab_ref[0] + up * tab_ref[1] + down * tab_ref[2]


def _silu(t):
  return t * jax.nn.sigmoid(t)


def _adaln_kernel(c_ref, w_ref, b_ref, o_ref):
  cs = _silu(c_ref[...])
  o_ref[0] = jnp.dot(cs, w_ref[0], preferred_element_type=F32,
                     precision=lax.Precision.HIGHEST) + b_ref[0]


def _adaln(c_pad, w_ada, b_ada):
  L, D, _ = w_ada.shape
  rows = c_pad.shape[0]
  return pl.pallas_call(
      _adaln_kernel,
      grid=(L, 3),
      in_specs=[
          pl.BlockSpec((rows, D), lambda l, j: (0, 0)),
          pl.BlockSpec((1, D, D), lambda l, j: (l, 0, j)),
          pl.BlockSpec((1, 1, D), lambda l, j: (l, 0, j)),
      ],
      out_specs=pl.BlockSpec((1, rows, D), lambda l, j: (l, 0, j)),
      out_shape=jax.ShapeDtypeStruct((L, rows, 3 * D), F32),
      compiler_params=_cparams(2),
      name="adaln",
  )(c_pad, w_ada, b_ada.reshape(L, 1, 3 * D))


def _inproj_kernel(x_ref, ng_ref, sh_ref, sc_ref, w_ref, gains_ref, taba_ref, tabb_ref, tabc_ref,
                   qa_ref, ka_ref, va_ref, ga_ref, qb_ref, kb_ref, vb_ref, gb_ref,
                   qc_ref, kc_ref, vc_ref, gc_ref, hb_ref):
  x = x_ref[...]
  ms = jnp.mean(x * x, axis=-1, keepdims=True)
  h = x * lax.rsqrt(ms + EPS) * ng_ref[...]
  h = h * (1.0 + sc_ref[...]) + sh_ref[...]
  hb_ref[...] = h.astype(BF16)

  def proj(c0, width):
    return jnp.dot(hb_ref[...], w_ref[:, c0:c0 + width], preferred_element_type=F32)

  def tiles(p):
    return [p[:, LANES * t:LANES * (t + 1)] for t in range(p.shape[1] // LANES)]

  gains = gains_ref[...]

  def normed_tiles(p, group, gain_row):
    y = p * lax.rsqrt(_group_sumsq(p, group) * (1.0 / group) + EPS)
    return [t * gain_row for t in tiles(y)]

  for t, p in enumerate(normed_tiles(proj(C_QA, W_A), DK_A, gains[0:1])):
    y = _rope_tile(p, taba_ref, ROT_A // 2) * (DK_A ** -0.5 * LOG2_E)
    qa_ref[LANES * t:LANES * (t + 1), :] = y.T.astype(BF16)
  for t, p in enumerate(normed_tiles(proj(C_KA, W_A), DK_A, gains[1:2])):
    y = _rope_tile(p, taba_ref, ROT_A // 2)
    ka_ref[:, LANES * t:LANES * (t + 1)] = y.astype(BF16)
  for t, p in enumerate(tiles(proj(C_VA, W_A))):
    va_ref[LANES * t:LANES * (t + 1), :] = p.T.astype(BF16)
  ga_ref[...] = _silu(proj(C_GA, W_A)).astype(BF16)

  for t, p in enumerate(normed_tiles(proj(C_QB, W_B), DH_B, gains[2:3])):
    qb_ref[t] = _rope_tile(p, tabb_ref, ROT_B // 2) * (DH_B ** -0.5 * LOG2_E)
  for t, p in enumerate(normed_tiles(proj(C_KB, W_B), DH_B, gains[3:4])):
    kb_ref[t] = _rope_tile(p, tabb_ref, ROT_B // 2)
  for t, p in enumerate(tiles(proj(C_VB, W_B))):
    vb_ref[t] = p
  gb_ref[...] = _silu(proj(C_GB, W_B)).astype(BF16)

  for t, p in enumerate(tiles(proj(C_QC, WQ_C))):
    qc_ref[:, LANES * t:LANES * (t + 1)] = _rope_tile(p, tabc_ref, DK_C // 2).astype(BF16)
  for t, p in enumerate(tiles(proj(C_KC, WQ_C))):
    y = _rope_tile(p, tabc_ref, DK_C // 2) * (DK_C ** -0.5)
    kc_ref[:, LANES * t:LANES * (t + 1)] = y.astype(BF16)
  vc_ref[...] = proj(C_VC, WV_C).astype(BF16)
  gc_ref[...] = _silu(proj(C_GC, WV_C)).astype(BF16)


def _inproj(x, ng, shift, scale, w, gains, taba, tabb, tabc):
  B, S, D = x.shape
  tm = min(ROW_TILE, S)
  n_pair = W_B // LANES

  def rows(width, dtype=BF16):
    return (pl.BlockSpec((None, tm, width), lambda i, b: (b, i, 0)),
            jax.ShapeDtypeStruct((B, S, width), dtype))

  def slabs():
    return (pl.BlockSpec((None, n_pair, tm, LANES), lambda i, b: (b, 0, i, 0)),
            jax.ShapeDtypeStruct((B, n_pair, S, LANES), F32))

  def feature_major(width):
    return (pl.BlockSpec((None, width, tm), lambda i, b: (b, 0, i)),
            jax.ShapeDtypeStruct((B, width, S), BF16))

  outs = [feature_major(W_A), rows(W_A), feature_major(W_A), rows(W_A),
          slabs(), slabs(), slabs(), rows(W_B),
          rows(WQ_C), rows(WQ_C), rows(WV_C), rows(WV_C)]
  tab_spec = pl.BlockSpec((3, tm, LANES), lambda i, b: (0, i, 0))
  return pl.pallas_call(
      _inproj_kernel,
      grid=(S // tm, B),
      in_specs=[
          pl.BlockSpec((None, tm, D), lambda i, b: (b, i, 0)),
          pl.BlockSpec((1, D), lambda i, b: (0, 0)),
          pl.BlockSpec((None, 1, D), lambda i, b: (b, 0, 0)),
          pl.BlockSpec((None, 1, D), lambda i, b: (b, 0, 0)),
          pl.BlockSpec((D, IN_COLS_P), lambda i, b: (0, 0)),
          pl.BlockSpec((4, LANES), lambda i, b: (0, 0)),
          tab_spec, tab_spec, tab_spec,
      ],
      out_specs=[o[0] for o in outs],
      out_shape=[o[1] for o in outs],
      scratch_shapes=[pltpu.VMEM((tm, D), BF16)],
      compiler_params=_cparams(2),
      name="inproj",
  )(x, ng, shift, scale, w, gains, taba, tabb, tabc)


def _attn_a_kernel(qt_ref, k_ref, vt_ref, g_ref, lam_ref, sub_ref, o_ref,
                   vx_ref, ot_ref, *, lam_init):
  S = k_ref.shape[0]
  nq = qt_ref.shape[1]
  tk = min(K_CHUNK_A, S)

  @pl.when(pl.program_id(1) == 0)
  def _():
    for h in range(H_A):
      vx_ref[h, 0:DV_A, :] = vt_ref[DV_A * h:DV_A * (h + 1), :]
      vx_ref[h, DV_A:VX_ROWS_A, :] = jnp.ones((VX_ROWS_A - DV_A, S), BF16)

  lp = lam_ref[...]
  lam = (jnp.exp(jnp.sum(lp[0:1] * lp[1:2], axis=-1, keepdims=True))
         - jnp.exp(jnp.sum(lp[2:3] * lp[3:4], axis=-1, keepdims=True)) + lam_init)
  row = lax.broadcasted_iota(jnp.int32, (W_A, 1), 0)

  qt = qt_ref[...]

  def masked_q(h, m):
    lo = h * DV_A + m * DK_A
    return jnp.where((row >= lo) & (row < lo + DK_A), qt, jnp.zeros_like(qt))

  def scores(unit):
    h, m, j = unit
    return jnp.dot(k_ref[j * tk:(j + 1) * tk, :], masked_q(h, m), preferred_element_type=F32)

  units = [(h, m, j) for h in range(H_A) for j in range(S // tk) for m in range(2)]
  pending = [scores(u) for u in units[:LOOKAHEAD_A]]
  run = {}
  for idx, (h, m, j) in enumerate(units):
    st = pending.pop(0)
    if idx + LOOKAHEAD_A < len(units):
      pending.append(scores(units[idx + LOOKAHEAD_A]))
    cm = jnp.max(st, axis=0, keepdims=True)
    vx = vx_ref[h, :, j * tk:(j + 1) * tk]
    m_new = cm if (h, m) not in run else jnp.maximum(run[h, m][0], cm)
    e = jnp.exp2(st - m_new).astype(BF16)
    pv = jnp.dot(vx, e, preferred_element_type=F32)
    if (h, m) not in run:
      run[h, m] = (m_new, pv)
    else:
      m_old, acc = run[h, m]
      run[h, m] = (m_new, jnp.exp2(m_old - m_new) * acc + pv)
    if m == 1 and j == S // tk - 1:
      maps = [run[h, mm][1] for mm in range(2)]
      maps = [acc[0:DV_A] / acc[DV_A:DV_A + 1] for acc in maps]
      ot_ref[DV_A * h:DV_A * (h + 1), :] = maps[0] - lam * maps[1]

  for t in range(W_A // LANES):
    cols = slice(LANES * t, LANES * (t + 1))
    o = ot_ref[cols, :].T
    y = o * lax.rsqrt(_group_sumsq(o, DV_A) * (1.0 / DV_A) + EPS) * sub_ref[:, cols]
    y = y * (1.0 - lam_init) * g_ref[:, cols].astype(F32)
    o_ref[:, cols] = y.astype(BF16)


def _attn_a(qat, ka, vat, ga, lam_params, sub, lam_init):
  B, S, _ = ka.shape
  tq = min(Q_TILE_A, S)
  return pl.pallas_call(
      functools.partial(_attn_a_kernel, lam_init=lam_init),
      grid=(B, S // tq),
      in_specs=[
          pl.BlockSpec((None, W_A, tq), lambda b, i: (b, 0, i)),
          pl.BlockSpec((None, S, W_A), lambda b, i: (b, 0, 0)),
          pl.BlockSpec((None, W_A, S), lambda b, i: (b, 0, 0)),
          pl.BlockSpec((None, tq, W_A), lambda b, i: (b, i, 0)),
          pl.BlockSpec((4, DK_A), lambda b, i: (0, 0)),
          pl.BlockSpec((1, W_A), lambda b, i: (0, 0)),
      ],
      out_specs=pl.BlockSpec((None, tq, W_A), lambda b, i: (b, i, 0)),
      out_shape=jax.ShapeDtypeStruct((B, S, W_A), BF16),
      scratch_shapes=[
          pltpu.VMEM((H_A, VX_ROWS_A, S), BF16),
          pltpu.VMEM((W_A, tq), F32),
      ],
      compiler_params=_cparams(2),
      name="attn_a",
  )(qat, ka, vat, ga, lam_params, sub)


def _attn_b_kernel(q_ref, k_ref, v_ref, g_ref, o_ref, m_ref, l_ref, acc_ref):
  S = q_ref.shape[0]
  lane = lax.broadcasted_iota(jnp.int32, (1, LANES), 1)
  first = lane < DH_B
  ii = lax.broadcasted_iota(jnp.int32, (2 * Q_BLOCK_B, K_WIN_B), 0) & (Q_BLOCK_B - 1)
  jj = lax.broadcasted_iota(jnp.int32, (2 * Q_BLOCK_B, K_WIN_B), 1)
  rel = jj - ii
  ones = jnp.ones((K_WIN_B, LANES), BF16)
  top, bot = slice(0, Q_BLOCK_B), slice(Q_BLOCK_B, 2 * Q_BLOCK_B)
  shape = (Q_BLOCK_B, LANES)

  for gi, d in enumerate(DILATIONS):
    class_len = S // d
    n_qblk = class_len // Q_BLOCK_B

    def items(ib, carry, d=d, gi=gi, class_len=class_len, n_qblk=n_qblk):
      loaded = []
      for u in range(ITEMS_B):
        i = ib * ITEMS_B + u
        r = i // n_qblk
        a0 = (i % n_qblk) * Q_BLOCK_B
        ak = jnp.clip(a0 - N_SIDE, 0, class_len - K_WIN_B)
        if d == 1:
          q_rows = pl.ds(pl.multiple_of(a0, Q_BLOCK_B), Q_BLOCK_B)
          k_rows = pl.ds(pl.multiple_of(ak, N_SIDE), K_WIN_B)
        else:
          q_rows = pl.ds(d * a0 + r, Q_BLOCK_B, stride=d)
          k_rows = pl.ds(d * ak + r, K_WIN_B, stride=d)
        qt = q_ref[q_rows, :]
        q2 = jnp.concatenate([jnp.where(first, qt, 0.0), jnp.where(first, 0.0, qt)], axis=0).astype(BF16)
        kt = k_ref[k_rows, :].astype(BF16)
        vt = jnp.concatenate([v_ref[k_rows, :].astype(BF16), ones], axis=1)
        loaded.append((q_rows, ak - a0, q2, kt, vt))
      scores = [lax.dot_general(q2, kt, _NT, preferred_element_type=F32) for _, _, q2, kt, _ in loaded]
      probs = []
      for (_, off, _, _, _), s in zip(loaded, scores):
        s = jnp.where(jnp.abs(rel + off) <= N_SIDE, s, NEG)
        m = jnp.max(s, axis=-1, keepdims=True)
        probs.append((m, jnp.exp2(s - m).astype(BF16)))
      outs = [jnp.dot(e, vt, preferred_element_type=F32) for (_, e), (_, _, _, _, vt) in zip(probs, loaded)]
      for (q_rows, _, _, _, _), (m, _), pv in zip(loaded, probs, outs):
        m_new = jnp.where(first, jnp.broadcast_to(m[top], shape), jnp.broadcast_to(m[bot], shape))
        l_new = jnp.where(first, pv[top, LANES:], pv[bot, LANES:])
        pv_new = jnp.where(first, pv[top, :LANES], pv[bot, :LANES])
        if gi == 0:
          m_ref[q_rows, :] = m_new
          l_ref[q_rows, :] = l_new
          acc_ref[q_rows, :] = pv_new
        else:
          m_old = m_ref[q_rows, :]
          m_tot = jnp.maximum(m_old, m_new)
          w_old = jnp.exp2(m_old - m_tot)
          w_new = jnp.exp2(m_new - m_tot)
          m_ref[q_rows, :] = m_tot
          l_ref[q_rows, :] = w_old * l_ref[q_rows, :] + w_new * l_new
          acc_ref[q_rows, :] = w_old * acc_ref[q_rows, :] + w_new * pv_new
      return carry

    lax.fori_loop(0, d * n_qblk // ITEMS_B, items, 0)

  def finish(c, carry):
    rows = pl.ds(pl.multiple_of(c * ROW_TILE, ROW_TILE), ROW_TILE)
    y = acc_ref[rows, :] / l_ref[rows, :] * g_ref[rows, :].astype(F32)
    o_ref[rows, :] = y.astype(BF16)
    return carry

  lax.fori_loop(0, S // ROW_TILE, finish, 0)


def _attn_b(qb, kb, vb, gb):
  B, n_pair, S, _ = qb.shape
  slab = pl.BlockSpec((None, None, S, LANES), lambda b, j: (b, j, 0, 0))
  cols = pl.BlockSpec((None, S, LANES), lambda b, j: (b, 0, j))
  return pl.pallas_call(
      _attn_b_kernel,
      grid=(B, n_pair),
      in_specs=[slab, slab, slab, cols],
      out_specs=cols,
      out_shape=jax.ShapeDtypeStruct((B, S, W_B), BF16),
      scratch_shapes=[pltpu.VMEM((S, LANES), F32)] * 3,
      compiler_params=_cparams(2),
      name="attn_b",
  )(qb, kb, vb, gb)


def _log_sigmoid(t):
  return jnp.minimum(t, 0.0) - jnp.log1p(jnp.exp(-jnp.abs(t)))


def _ret_kernel(q_ref, k_ref, v_ref, g_ref, rdq_ref, rdh_ref, gn_ref, o_ref, u_ref, d_ref, tab_ref):
  S = q_ref.shape[0]
  C = RET_CHUNK
  N = S // C
  lgq = _log_sigmoid(rdq_ref[...])
  lgh = _log_sigmoid(rdh_ref[...])
  pos = lax.broadcasted_iota(jnp.int32, (C, 1), 0).astype(F32)
  tab_ref[0] = jnp.exp(lgq[0:1] * (C - 1.0 - pos))
  tab_ref[1] = jnp.exp(lgq[1:2] * pos)
  tab_ref[2] = jnp.exp(lgq[0:1] * (pos + 1.0))
  tab_ref[3] = jnp.exp(lgq[1:2] * (C - pos))
  dd = (lax.broadcasted_iota(jnp.int32, (C, C), 0) - lax.broadcasted_iota(jnp.int32, (C, C), 1))
  dist = jnp.abs(dd).astype(F32)
  for h in range(H_C):
    d_ref[h] = jnp.exp(jnp.where(dd >= 0, lgh[h:h + 1], lgh[H_C + h:H_C + h + 1]) * dist)
  cdec = jnp.exp(lgh[:, 0:LANES] * float(C))

  lane = lax.broadcasted_iota(jnp.int32, (1, LANES), 1)
  first = lane < DKP_C

  def stack(fwd, bwd, hh):
    if hh == 0:
      return jnp.where(first, fwd, pltpu.roll(bwd, DKP_C, 1))
    return jnp.where(first, pltpu.roll(fwd, DKP_C, 1), bwd)

  def chunk_rows(n):
    return pl.ds(pl.multiple_of(n * C, C), C)

  def local_state(n, carry):
    rows = chunk_rows(n)
    for t in range(WQ_C // LANES):
      cols = slice(LANES * t, LANES * (t + 1))
      kt = k_ref[rows, cols].astype(F32)
      kf = kt * tab_ref[0, :, cols]
      kb = kt * tab_ref[1, :, cols]
      for hh in range(2):
        h = 2 * t + hh
        st = stack(kf, kb, hh).T.astype(BF16)
        u_ref[n, h] = jnp.dot(st, v_ref[rows, DVP_C * h:DVP_C * (h + 1)], preferred_element_type=F32)
    return carry

  lax.fori_loop(0, N, local_state, 0)

  zero = tuple(jnp.zeros((DKP_C, DVP_C), F32) for _ in range(H_C))

  def scan_fwd(n, state):
    new = []
    for h in range(H_C):
      u = u_ref[n, h, 0:DKP_C, :]
      u_ref[n, h, 0:DKP_C, :] = state[h]
      new.append(state[h] * cdec[h:h + 1] + u)
    return tuple(new)

  lax.fori_loop(0, N, scan_fwd, zero)

  def scan_bwd(i, state):
    n = N - 1 - i
    new = []
    for h in range(H_C):
      u = u_ref[n, h, DKP_C:2 * DKP_C, :]
      u_ref[n, h, DKP_C:2 * DKP_C, :] = state[h]
      new.append(state[h] * cdec[H_C + h:H_C + h + 1] + u)
    return tuple(new)

  lax.fori_loop(0, N, scan_bwd, zero)

  def outputs(n, carry):
    rows = chunk_rows(n)
    for t in range(WQ_C // LANES):
      cols = slice(LANES * t, LANES * (t + 1))
      qt = q_ref[rows, cols]
      kt = k_ref[rows, cols]
      q32 = qt.astype(F32)
      qf = q32 * tab_ref[2, :, cols]
      qb = q32 * tab_ref[3, :, cols]
      for hh in range(2):
        h = 2 * t + hh
        vcols = slice(DVP_C * h, DVP_C * (h + 1))
        keep = first if hh == 0 else jnp.logical_not(first)
        qm = jnp.where(keep, qt, jnp.zeros_like(qt))
        att = lax.dot_general(qm, kt, _NT, preferred_element_type=F32) * d_ref[h]
        inner = jnp.dot(att.astype(BF16), v_ref[rows, vcols], preferred_element_type=F32)
        cross = jnp.dot(stack(qf, qb, hh).astype(BF16), u_ref[n, h].astype(BF16),
                        preferred_element_type=F32)
        o = inner + cross
        ms = jnp.sum(o * o, axis=-1, keepdims=True) * (1.0 / DV_C)
        y = o * lax.rsqrt(ms + EPS) * gn_ref[...]
        o_ref[rows, vcols] = (y * g_ref[rows, vcols].astype(F32)).astype(BF16)
    return carry

  lax.fori_loop(0, N, outputs, 0)


def _retention(qc, kc, vc, gc, rdq, rdh, gn):
  B, S, _ = qc.shape
  C = RET_CHUNK
  return pl.pallas_call(
      _ret_kernel,
      grid=(B,),
      in_specs=[
          pl.BlockSpec((None, S, WQ_C), lambda b: (b, 0, 0)),
          pl.BlockSpec((None, S, WQ_C), lambda b: (b, 0, 0)),
          pl.BlockSpec((None, S, WV_C), lambda b: (b, 0, 0)),
          pl.BlockSpec((None, S, WV_C), lambda b: (b, 0, 0)),
          pl.BlockSpec((2, WQ_C), lambda b: (0, 0)),
          pl.BlockSpec((2 * H_C, WQ_C), lambda b: (0, 0)),
          pl.BlockSpec((1, DVP_C), lambda b: (0, 0)),
      ],
      out_specs=pl.BlockSpec((None, S, WV_C), lambda b: (b, 0, 0)),
      out_shape=jax.ShapeDtypeStruct((B, S, WV_C), BF16),
      scratch_shapes=[
          pltpu.VMEM((S // C, H_C, 2 * DKP_C, DVP_C), F32),
          pltpu.VMEM((H_C, C, C), F32),
          pltpu.VMEM((4, C, WQ_C), F32),
      ],
      compiler_params=_cparams(1),
      name="retention",
  )(qc, kc, vc, gc, rdq, rdh, gn)


def _outproj_kernel(oa_ref, ob_ref, oc_ref, w_ref, x_ref, gate_ref, o_ref):
  y = jnp.dot(oa_ref[...], w_ref[0:W_A, :], preferred_element_type=F32)
  y += jnp.dot(ob_ref[...], w_ref[W_A:W_A + W_B, :], preferred_element_type=F32)
  y += jnp.dot(oc_ref[...], w_ref[W_A + W_B:OUT_ROWS_P, :], preferred_element_type=F32)
  o_ref[...] = x_ref[...] + gate_ref[...] * y


def _outproj(oa, ob, oc, w, x, gate):
  B, S, D = x.shape
  tm = min(ROW_TILE, S)

  def rows(width):
    return pl.BlockSpec((None, tm, width), lambda b, i: (b, i, 0))

  return pl.pallas_call(
      _outproj_kernel,
      grid=(B, S // tm),
      in_specs=[rows(W_A), rows(W_B), rows(WV_C),
                pl.BlockSpec((OUT_ROWS_P, D), lambda b, i: (0, 0)),
                rows(D),
                pl.BlockSpec((None, 1, D), lambda b, i: (b, 0, 0))],
      out_specs=rows(D),
      out_shape=jax.ShapeDtypeStruct((B, S, D), F32),
      compiler_params=_cparams(2),
      name="outproj",
  )(oa, ob, oc, w, x, gate)


def _rope_tables(S, theta, rot_dim, group):
  half = rot_dim // 2
  pos = jnp.arange(S, dtype=F32)
  inv = theta ** (-jnp.arange(0, rot_dim, 2, dtype=F32) / rot_dim)
  ang = pos[:, None] * inv[None, :]
  cos, sin = jnp.cos(ang), jnp.sin(ang)
  g = np.arange(LANES) % group
  idx = g % half
  in_first = jnp.asarray(g < half)[None, :]
  in_second = jnp.asarray((g >= half) & (g < rot_dim))[None, :]
  c = jnp.where(in_first | in_second, cos[:, idx], 1.0)
  s1 = jnp.where(in_first, -sin[:, idx], 0.0)
  s2 = jnp.where(in_second, sin[:, idx], 0.0)
  return jnp.stack([c, s1, s2]).astype(F32)


def _pad_heads(w, n_heads, d, d_pad, axis):
  shape = w.shape[:axis] + (n_heads, d) + w.shape[axis + 1:]
  pad = [(0, 0)] * (len(shape))
  pad[axis + 1] = (0, d_pad - d)
  out = jnp.pad(w.reshape(shape), pad)
  return out.reshape(w.shape[:axis] + (n_heads * d_pad,) + w.shape[axis + 1:])


def _prep_w_in(w_in):
  offs = np.concatenate([[0], np.cumsum(SPLIT_SIZES)])
  sec = [w_in[..., offs[i]:offs[i + 1]] for i in range(len(SPLIT_SIZES))]
  sec[8] = _pad_heads(sec[8], H_C, DK_C, DKP_C, w_in.ndim - 1)
  sec[9] = _pad_heads(sec[9], H_C, DK_C, DKP_C, w_in.ndim - 1)
  sec[10] = _pad_heads(sec[10], H_C, DV_C, DVP_C, w_in.ndim - 1)
  sec[11] = _pad_heads(sec[11], H_C, DV_C, DVP_C, w_in.ndim - 1)
  return jnp.concatenate(sec, axis=-1).astype(BF16)


def _prep_w_out(w_out):
  top = w_out[:, :W_A + W_B]
  bot = _pad_heads(w_out[:, W_A + W_B:], H_C, DV_C, DVP_C, 1)
  return jnp.concatenate([top, bot], axis=1).astype(BF16)


@jax.jit
def _forward(x, c, norm_g, w_ada, b_ada, w_in, w_out, qn_a, kn_a, lambda_q1, lambda_k1,
             lambda_q2, lambda_k2, subln_a, qn_b, kn_b, ret_decay, gn_c):
  B, S, D = x.shape
  L = w_in.shape[0]
  rows = 8 * ((B + 7) // 8)
  c_pad = jnp.pad(c, ((0, rows - B), (0, 0)))
  mod = _adaln(c_pad, w_ada, b_ada)[:, :B]
  w_in_p = _prep_w_in(w_in)
  w_out_p = _prep_w_out(w_out)
  taba = _rope_tables(S, ROT_THETA, ROT_A, DK_A)
  tabb = _rope_tables(S, ROT_THETA, ROT_B, DH_B)
  tabc = _rope_tables(S, RET_THETA, DK_C, DKP_C)
  gn_p = jnp.pad(gn_c, ((0, 0), (0, DVP_C - DV_C)))
  for l in range(L):
    shift = mod[l, :, None, 0:D]
    scale = mod[l, :, None, D:2 * D]
    gate = mod[l, :, None, 2 * D:3 * D]
    gains = jnp.stack([jnp.tile(qn_a[l], LANES // DK_A), jnp.tile(kn_a[l], LANES // DK_A),
                       jnp.tile(qn_b[l], LANES // DH_B), jnp.tile(kn_b[l], LANES // DH_B)])
    (qa, ka, va, ga, qb, kb, vb, gb, qc, kc, vc, gc) = _inproj(
        x, norm_g[l][None, :], shift, scale, w_in_p[l], gains, taba, tabb, tabc)
    lam_init = 0.8 - 0.6 * math.exp(-0.3 * l)
    lam_params = jnp.stack([lambda_q1[l], lambda_k1[l], lambda_q2[l], lambda_k2[l]])
    oa = _attn_a(qa, ka, va, ga, lam_params, jnp.tile(subln_a[l], H_A)[None, :], lam_init)
    ob = _attn_b(qb, kb, vb, gb)
    rdq = jnp.repeat(ret_decay[l], DKP_C, axis=1)
    rdh = jnp.broadcast_to(ret_decay[l].reshape(2 * H_C, 1), (2 * H_C, WQ_C))
    oc = _retention(qc, kc, vc, gc, rdq, rdh, gn_p[l][None, :])
    x = _outproj(oa, ob, oc, w_out_p[l], x, gate)
  return x


def kernel(x, c, norm_g, w_ada, b_ada, w_in, w_out, qn_a, kn_a, lambda_q1, lambda_k1, lambda_q2,
           lambda_k2, subln_a, qn_b, kn_b, ret_decay, gn_c):
  return _forward(x, c, norm_g, w_ada, b_ada, w_in, w_out, qn_a, kn_a, lambda_q1, lambda_k1,
                  lambda_q2, lambda_k2, subln_a, qn_b, kn_b, ret_decay, gn_c)
```

```python
import functools
import math

import numpy as np
import jax
import jax.numpy as jnp
from jax import lax
from jax.experimental import pallas as pl
from jax.experimental.pallas import tpu as pltpu

F32 = jnp.float32
BF16 = jnp.bfloat16

H_A, DK_A, DV_A = 4, 32, 64
H_B, DH_B = 6, 64
H_C, DK_C, DV_C = 4, 48, 96
SPLIT_SIZES = (256, 256, 256, 256, 384, 384, 384, 384, 192, 192, 384, 384)
ROT_THETA = 500000.0
ROT_A = DK_A // 4
ROT_B = DH_B // 4
RET_THETA = 10000.0
N_SIDE = 64
DILATIONS = (1, 4, 16)
EPS = 1e-6
NEG = -1e30

LANES = 128
MXU_DIM = 256
V7X_VMEM_BYTES = 64 * 1024 * 1024
VMEM_LIMIT = V7X_VMEM_BYTES * 7 // 8

DKP_C = 64
DVP_C = LANES
W_A = H_A * DV_A
W_B = H_B * DH_B
WQ_C = H_C * DKP_C
WV_C = H_C * DVP_C
C_QA, C_KA, C_VA, C_GA = 0, 256, 512, 768
C_QB, C_KB, C_VB, C_GB = 1024, 1408, 1792, 2176
C_QC, C_KC, C_VC, C_GC = 2560, 2816, 3072, 3584
IN_COLS_P = 4096
OUT_ROWS_P = W_A + W_B + WV_C

RET_CHUNK = 256
Q_TILE_A = 256
K_CHUNK_A = 256
LOOKAHEAD_A = 6
VX_ROWS_A = DV_A + 16
LOG2_E = math.log2(math.e)
Q_BLOCK_B = 128
K_WIN_B = Q_BLOCK_B + 2 * N_SIDE
WINDOW_OFFSETS_B = (-N_SIDE, 0, -2 * N_SIDE)
ITEMS_B = 4
ROW_TILE = 512

_NT = (((1,), (1,)), ((), ()))


def _cparams(n_grid):
  return pltpu.CompilerParams(
      dimension_semantics=("arbitrary",) * n_grid, vmem_limit_bytes=VMEM_LIMIT)


def _group_ones(group, width=LANES):
  shift = int(math.log2(group))
  r = lax.broadcasted_iota(jnp.int32, (width, width), 0) >> shift
  c = lax.broadcasted_iota(jnp.int32, (width, width), 1) >> shift
  return (r == c).astype(BF16)


def _group_sumsq(t, group):
  t2 = (t * t).astype(BF16)
  parts = []
  for c0 in range(0, t.shape[1], MXU_DIM):
    w = min(MXU_DIM, t.shape[1] - c0)
    parts.append(jnp.dot(t2[:, c0:c0 + w], _group_ones(group, w), preferred_element_type=F32))
  return parts[0] if len(parts) == 1 else jnp.concatenate(parts, axis=1)


def _rope_tile(t, tab_ref, half):
  up = pltpu.roll(t, LANES - half, 1)
  down = pltpu.roll(t, half, 1)
  return t * tab_ref[0] + up * tab_ref[1] + down * tab_ref[2]


def _silu(t):
  return t * jax.nn.sigmoid(t)


def _adaln_kernel(c_ref, w_ref, b_ref, o_ref):
  cs = _silu(c_ref[...])
  o_ref[0] = jnp.dot(cs, w_ref[0], preferred_element_type=F32,
                     precision=lax.Precision.HIGHEST) + b_ref[0]


def _adaln(c_pad, w_ada, b_ada):
  L, D, _ = w_ada.shape
  rows = c_pad.shape[0]
  return pl.pallas_call(
      _adaln_kernel,
      grid=(L, 3),
      in_specs=[
          pl.BlockSpec((rows, D), lambda l, j: (0, 0)),
          pl.BlockSpec((1, D, D), lambda l, j: (l, 0, j)),
          pl.BlockSpec((1, 1, D), lambda l, j: (l, 0, j)),
      ],
      out_specs=pl.BlockSpec((1, rows, D), lambda l, j: (l, 0, j)),
      out_shape=jax.ShapeDtypeStruct((L, rows, 3 * D), F32),
      compiler_params=_cparams(2),
      name="adaln",
  )(c_pad, w_ada, b_ada.reshape(L, 1, 3 * D))


def _inproj_kernel(x_ref, ng_ref, sh_ref, sc_ref, w_ref, gains_ref, taba_ref, tabb_ref, tabc_ref,
                   qa_ref, ka_ref, va_ref, ga_ref, qb_ref, kb_ref, vb_ref, gb_ref,
                   qc_ref, kc_ref, vc_ref, gc_ref, hb_ref):
  x = x_ref[...]
  ms = jnp.mean(x * x, axis=-1, keepdims=True)
  h = x * lax.rsqrt(ms + EPS) * ng_ref[...]
  h = h * (1.0 + sc_ref[...]) + sh_ref[...]
  hb_ref[...] = h.astype(BF16)

  def proj(c0, width):
    return jnp.dot(hb_ref[...], w_ref[:, c0:c0 + width], preferred_element_type=F32)

  def tiles(p):
    return [p[:, LANES * t:LANES * (t + 1)] for t in range(p.shape[1] // LANES)]

  gains = gains_ref[...]

  def normed_tiles(p, group, gain_row):
    y = p * lax.rsqrt(_group_sumsq(p, group) * (1.0 / group) + EPS)
    return [t * gain_row for t in tiles(y)]

  for t, p in enumerate(normed_tiles(proj(C_QA, W_A), DK_A, gains[0:1])):
    y = _rope_tile(p, taba_ref, ROT_A // 2) * (DK_A ** -0.5 * LOG2_E)
    qa_ref[LANES * t:LANES * (t + 1), :] = y.T.astype(BF16)
  for t, p in enumerate(normed_tiles(proj(C_KA, W_A), DK_A, gains[1:2])):
    y = _rope_tile(p, taba_ref, ROT_A // 2)
    ka_ref[:, LANES * t:LANES * (t + 1)] = y.astype(BF16)
  for t, p in enumerate(tiles(proj(C_VA, W_A))):
    va_ref[LANES * t:LANES * (t + 1), :] = p.T.astype(BF16)
  ga_ref[...] = _silu(proj(C_GA, W_A)).astype(BF16)

  for t, p in enumerate(normed_tiles(proj(C_QB, W_B), DH_B, gains[2:3])):
    qb_ref[t] = _rope_tile(p, tabb_ref, ROT_B // 2) * (DH_B ** -0.5 * LOG2_E)
  for t, p in enumerate(normed_tiles(proj(C_KB, W_B), DH_B, gains[3:4])):
    kb_ref[t] = _rope_tile(p, tabb_ref, ROT_B // 2)
  for t, p in enumerate(tiles(proj(C_VB, W_B))):
    vb_ref[t] = p
  gb_ref[...] = _silu(proj(C_GB, W_B)).astype(BF16)

  for t, p in enumerate(tiles(proj(C_QC, WQ_C))):
    qc_ref[:, LANES * t:LANES * (t + 1)] = _rope_tile(p, tabc_ref, DK_C // 2).astype(BF16)
  for t, p in enumerate(tiles(proj(C_KC, WQ_C))):
    y = _rope_tile(p, tabc_ref, DK_C // 2) * (DK_C ** -0.5)
    kc_ref[:, LANES * t:LANES * (t + 1)] = y.astype(BF16)
  vc_ref[...] = proj(C_VC, WV_C).astype(BF16)
  gc_ref[...] = _silu(proj(C_GC, WV_C)).astype(BF16)


def _inproj(x, ng, shift, scale, w, gains, taba, tabb, tabc):
  B, S, D = x.shape
  tm = min(ROW_TILE, S)
  n_pair = W_B // LANES

  def rows(width, dtype=BF16):
    return (pl.BlockSpec((None, tm, width), lambda i, b: (b, i, 0)),
            jax.ShapeDtypeStruct((B, S, width), dtype))

  def slabs():
    return (pl.BlockSpec((None, n_pair, tm, LANES), lambda i, b: (b, 0, i, 0)),
            jax.ShapeDtypeStruct((B, n_pair, S, LANES), F32))

  def feature_major(width):
    return (pl.BlockSpec((None, width, tm), lambda i, b: (b, 0, i)),
            jax.ShapeDtypeStruct((B, width, S), BF16))

  outs = [feature_major(W_A), rows(W_A), feature_major(W_A), rows(W_A),
          slabs(), slabs(), slabs(), rows(W_B),
          rows(WQ_C), rows(WQ_C), rows(WV_C), rows(WV_C)]
  tab_spec = pl.BlockSpec((3, tm, LANES), lambda i, b: (0, i, 0))
  return pl.pallas_call(
      _inproj_kernel,
      grid=(S // tm, B),
      in_specs=[
          pl.BlockSpec((None, tm, D), lambda i, b: (b, i, 0)),
          pl.BlockSpec((1, D), lambda i, b: (0, 0)),
          pl.BlockSpec((None, 1, D), lambda i, b: (b, 0, 0)),
          pl.BlockSpec((None, 1, D), lambda i, b: (b, 0, 0)),
          pl.BlockSpec((D, IN_COLS_P), lambda i, b: (0, 0)),
          pl.BlockSpec((4, LANES), lambda i, b: (0, 0)),
          tab_spec, tab_spec, tab_spec,
      ],
      out_specs=[o[0] for o in outs],
      out_shape=[o[1] for o in outs],
      scratch_shapes=[pltpu.VMEM((tm, D), BF16)],
      compiler_params=_cparams(2),
      name="inproj",
  )(x, ng, shift, scale, w, gains, taba, tabb, tabc)


def _attn_a_kernel(qt_ref, k_ref, vt_ref, g_ref, lam_ref, sub_ref, o_ref,
                   vx_ref, ot_ref, *, lam_init):
  S = k_ref.shape[0]
  nq = qt_ref.shape[1]
  tk = min(K_CHUNK_A, S)

  @pl.when(pl.program_id(1) == 0)
  def _():
    for h in range(H_A):
      vx_ref[h, 0:DV_A, :] = vt_ref[DV_A * h:DV_A * (h + 1), :]
      vx_ref[h, DV_A:VX_ROWS_A, :] = jnp.ones((VX_ROWS_A - DV_A, S), BF16)

  lp = lam_ref[...]
  lam = (jnp.exp(jnp.sum(lp[0:1] * lp[1:2], axis=-1, keepdims=True))
         - jnp.exp(jnp.sum(lp[2:3] * lp[3:4], axis=-1, keepdims=True)) + lam_init)
  row = lax.broadcasted_iota(jnp.int32, (W_A, 1), 0)

  qt = qt_ref[...]

  def masked_q(h, m):
    lo = h * DV_A + m * DK_A
    return jnp.where((row >= lo) & (row < lo + DK_A), qt, jnp.zeros_like(qt))

  def scores(unit):
    h, m, j = unit
    return jnp.dot(k_ref[j * tk:(j + 1) * tk, :], masked_q(h, m), preferred_element_type=F32)

  units = [(h, m, j) for h in range(H_A) for j in range(S // tk) for m in range(2)]
  pending = [scores(u) for u in units[:LOOKAHEAD_A]]
  run = {}
  for idx, (h, m, j) in enumerate(units):
    st = pending.pop(0)
    if idx + LOOKAHEAD_A < len(units):
      pending.append(scores(units[idx + LOOKAHEAD_A]))
    cm = jnp.max(st, axis=0, keepdims=True)
    vx = vx_ref[h, :, j * tk:(j + 1) * tk]
    m_new = cm if (h, m) not in run else jnp.maximum(run[h, m][0], cm)
    e = jnp.exp2(st - m_new).astype(BF16)
    pv = jnp.dot(vx, e, preferred_element_type=F32)
    if (h, m) not in run:
      run[h, m] = (m_new, pv)
    else:
      m_old, acc = run[h, m]
      run[h, m] = (m_new, jnp.exp2(m_old - m_new) * acc + pv)
    if m == 1 and j == S // tk - 1:
      maps = [run[h, mm][1] for mm in range(2)]
      maps = [acc[0:DV_A] / acc[DV_A:DV_A + 1] for acc in maps]
      ot_ref[DV_A * h:DV_A * (h + 1), :] = maps[0] - lam * maps[1]

  for t in range(W_A // LANES):
    cols = slice(LANES * t, LANES * (t + 1))
    o = ot_ref[cols, :].T
    y = o * lax.rsqrt(_group_sumsq(o, DV_A) * (1.0 / DV_A) + EPS) * sub_ref[:, cols]
    y = y * (1.0 - lam_init) * g_ref[:, cols].astype(F32)
    o_ref[:, cols] = y.astype(BF16)


def _attn_a(qat, ka, vat, ga, lam_params, sub, lam_init):
  B, S, _ = ka.shape
  tq = min(Q_TILE_A, S)
  return pl.pallas_call(
      functools.partial(_attn_a_kernel, lam_init=lam_init),
      grid=(B, S // tq),
      in_specs=[
          pl.BlockSpec((None, W_A, tq), lambda b, i: (b, 0, i)),
          pl.BlockSpec((None, S, W_A), lambda b, i: (b, 0, 0)),
          pl.BlockSpec((None, W_A, S), lambda b, i: (b, 0, 0)),
          pl.BlockSpec((None, tq, W_A), lambda b, i: (b, i, 0)),
          pl.BlockSpec((4, DK_A), lambda b, i: (0, 0)),
          pl.BlockSpec((1, W_A), lambda b, i: (0, 0)),
      ],
      out_specs=pl.BlockSpec((None, tq, W_A), lambda b, i: (b, i, 0)),
      out_shape=jax.ShapeDtypeStruct((B, S, W_A), BF16),
      scratch_shapes=[
          pltpu.VMEM((H_A, VX_ROWS_A, S), BF16),
          pltpu.VMEM((W_A, tq), F32),
      ],
      compiler_params=_cparams(2),
      name="attn_a",
  )(qat, ka, vat, ga, lam_params, sub)


def _attn_b_kernel(q_ref, k_ref, v_ref, g_ref, o_ref, m_ref, l_ref, acc_ref, bias_ref):
  S = q_ref.shape[0]
  lane = lax.broadcasted_iota(jnp.int32, (1, LANES), 1)
  first = lane < DH_B
  ones = jnp.ones((K_WIN_B, LANES), BF16)

  def band(n_q, off):
    qi = lax.broadcasted_iota(jnp.int32, (2 * n_q, K_WIN_B), 0) & (n_q - 1)
    kj = lax.broadcasted_iota(jnp.int32, (2 * n_q, K_WIN_B), 1)
    return jnp.where(jnp.abs(kj - qi + off) <= N_SIDE, 0.0, NEG).astype(F32)

  for idx, off in enumerate(WINDOW_OFFSETS_B):
    bias_ref[idx, 0:2 * Q_BLOCK_B, :] = band(Q_BLOCK_B, off)
  bias_ref[len(WINDOW_OFFSETS_B)] = band(K_WIN_B, 0)

  for gi, d in enumerate(sorted(DILATIONS, reverse=True)):
    class_len = S // d
    qblk = class_len if class_len <= K_WIN_B else Q_BLOCK_B
    n_qblk = class_len // qblk
    n_items = min(ITEMS_B * Q_BLOCK_B // qblk, d * n_qblk)
    top, bot = slice(0, qblk), slice(qblk, 2 * qblk)
    shape = (qblk, LANES)

    def items(ib, carry, d=d, gi=gi, class_len=class_len, qblk=qblk, n_qblk=n_qblk, n_items=n_items,
              top=top, bot=bot, shape=shape):
      loaded = []
      for u in range(n_items):
        i = ib * n_items + u
        r = i // n_qblk
        a0 = (i % n_qblk) * qblk
        ak = jnp.clip(a0 - N_SIDE, 0, class_len - K_WIN_B)
        if d == 1:
          q_rows = pl.ds(pl.multiple_of(a0, qblk), qblk)
          k_rows = pl.ds(pl.multiple_of(ak, N_SIDE), K_WIN_B)
        else:
          q_rows = pl.ds(d * a0 + r, qblk, stride=d)
          k_rows = pl.ds(d * ak + r, K_WIN_B, stride=d)
        qt = q_ref[q_rows, :]
        q2 = jnp.concatenate([jnp.where(first, qt, 0.0), jnp.where(first, 0.0, qt)], axis=0).astype(BF16)
        kt = k_ref[k_rows, :].astype(BF16)
        vt = jnp.concatenate([v_ref[k_rows, :].astype(BF16), ones], axis=1)
        if qblk == K_WIN_B:
          which = len(WINDOW_OFFSETS_B)
        else:
          which = jnp.where(a0 == 0, 1, jnp.where(a0 == class_len - qblk, 2, 0))
        loaded.append((q_rows, which, q2, kt, vt))
      scores = [lax.dot_general(q2, kt, _NT, preferred_element_type=F32) for _, _, q2, kt, _ in loaded]
      probs = []
      for (_, which, _, _, _), s in zip(loaded, scores):
        s = s + bias_ref[which, 0:2 * qblk, :]
        m = jnp.max(s, axis=-1, keepdims=True)
        probs.append((m, jnp.exp2(s - m).astype(BF16)))
      outs = [jnp.dot(e, vt, preferred_element_type=F32) for (_, e), (_, _, _, _, vt) in zip(probs, loaded)]
      for (q_rows, _, _, _, _), (m, _), pv in zip(loaded, probs, outs):
        m_new = jnp.where(first, jnp.broadcast_to(m[top], shape), jnp.broadcast_to(m[bot], shape))
        l_new = jnp.where(first, pv[top, LANES:], pv[bot, LANES:])
        pv_new = jnp.where(first, pv[top, :LANES], pv[bot, :LANES])
        if gi == 0:
          m_ref[q_rows, :] = m_new
          l_ref[q_rows, :] = l_new
          acc_ref[q_rows, :] = pv_new
        else:
          m_old = m_ref[q_rows, :]
          m_tot = jnp.maximum(m_old, m_new)
          w_old = jnp.exp2(m_old - m_tot)
          w_new = jnp.exp2(m_new - m_tot)
          m_ref[q_rows, :] = m_tot
          l_ref[q_rows, :] = w_old * l_ref[q_rows, :] + w_new * l_new
          acc_ref[q_rows, :] = w_old * acc_ref[q_rows, :] + w_new * pv_new
      return carry

    lax.fori_loop(0, d * n_qblk // n_items, items, 0)

  def finish(c, carry):
    rows = pl.ds(pl.multiple_of(c * ROW_TILE, ROW_TILE), ROW_TILE)
    y = acc_ref[rows, :] / l_ref[rows, :] * g_ref[rows, :].astype(F32)
    o_ref[rows, :] = y.astype(BF16)
    return carry

  lax.fori_loop(0, S // ROW_TILE, finish, 0)


def _attn_b(qb, kb, vb, gb):
  B, n_pair, S, _ = qb.shape
  slab = pl.BlockSpec((None, None, S, LANES), lambda b, j: (b, j, 0, 0))
  cols = pl.BlockSpec((None, S, LANES), lambda b, j: (b, 0, j))
  return pl.pallas_call(
      _attn_b_kernel,
      grid=(B, n_pair),
      in_specs=[slab, slab, slab, cols],
      out_specs=cols,
      out_shape=jax.ShapeDtypeStruct((B, S, W_B), BF16),
      scratch_shapes=[pltpu.VMEM((S, LANES), F32)] * 3
      + [pltpu.VMEM((len(WINDOW_OFFSETS_B) + 1, 2 * K_WIN_B, K_WIN_B), F32)],
      compiler_params=_cparams(2),
      name="attn_b",
  )(qb, kb, vb, gb)


def _log_sigmoid(t):
  return jnp.minimum(t, 0.0) - jnp.log1p(jnp.exp(-jnp.abs(t)))


def _ret_kernel(q_ref, k_ref, v_ref, g_ref, rdq_ref, rdh_ref, gn_ref, o_ref, u_ref, d_ref, tab_ref):
  S = q_ref.shape[0]
  C = RET_CHUNK
  N = S // C
  lgq = _log_sigmoid(rdq_ref[...])
  lgh = _log_sigmoid(rdh_ref[...])
  pos = lax.broadcasted_iota(jnp.int32, (C, 1), 0).astype(F32)
  tab_ref[0] = jnp.exp(lgq[0:1] * (C - 1.0 - pos))
  tab_ref[1] = jnp.exp(lgq[1:2] * pos)
  tab_ref[2] = jnp.exp(lgq[0:1] * (pos + 1.0))
  tab_ref[3] = jnp.exp(lgq[1:2] * (C - pos))
  dd = (lax.broadcasted_iota(jnp.int32, (C, C), 0) - lax.broadcasted_iota(jnp.int32, (C, C), 1))
  dist = jnp.abs(dd).astype(F32)
  for h in range(H_C):
    d_ref[h] = jnp.exp(jnp.where(dd >= 0, lgh[h:h + 1], lgh[H_C + h:H_C + h + 1]) * dist)
  cdec = jnp.exp(lgh[:, 0:LANES] * float(C))

  lane = lax.broadcasted_iota(jnp.int32, (1, LANES), 1)
  first = lane < DKP_C

  def stack(fwd, bwd, hh):
    if hh == 0:
      return jnp.where(first, fwd, pltpu.roll(bwd, DKP_C, 1))
    return jnp.where(first, pltpu.roll(fwd, DKP_C, 1), bwd)

  def chunk_rows(n):
    return pl.ds(pl.multiple_of(n * C, C), C)

  def local_state(n, carry):
    rows = chunk_rows(n)
    stacks = []
    for t in range(WQ_C // LANES):
      cols = slice(LANES * t, LANES * (t + 1))
      kt = k_ref[rows, cols].astype(F32)
      kf = kt * tab_ref[0, :, cols]
      kb = kt * tab_ref[1, :, cols]
      stacks += [stack(kf, kb, hh).T.astype(BF16) for hh in range(2)]
    states = [jnp.dot(st, v_ref[rows, DVP_C * h:DVP_C * (h + 1)], preferred_element_type=F32)
              for h, st in enumerate(stacks)]
    for h, u in enumerate(states):
      u_ref[n, h] = u
    return carry

  lax.fori_loop(0, N, local_state, 0)

  zero = tuple(jnp.zeros((DKP_C, DVP_C), F32) for _ in range(H_C))

  def scan_fwd(n, state):
    new = []
    for h in range(H_C):
      u = u_ref[n, h, 0:DKP_C, :]
      u_ref[n, h, 0:DKP_C, :] = state[h]
      new.append(state[h] * cdec[h:h + 1] + u)
    return tuple(new)

  lax.fori_loop(0, N, scan_fwd, zero)

  def scan_bwd(i, state):
    n = N - 1 - i
    new = []
    for h in range(H_C):
      u = u_ref[n, h, DKP_C:2 * DKP_C, :]
      u_ref[n, h, DKP_C:2 * DKP_C, :] = state[h]
      new.append(state[h] * cdec[H_C + h:H_C + h + 1] + u)
    return tuple(new)

  lax.fori_loop(0, N, scan_bwd, zero)

  def outputs(n, carry):
    rows = chunk_rows(n)
    qms, kts, qss = [], [], []
    for t in range(WQ_C // LANES):
      cols = slice(LANES * t, LANES * (t + 1))
      qt = q_ref[rows, cols]
      q32 = qt.astype(F32)
      qf = q32 * tab_ref[2, :, cols]
      qb = q32 * tab_ref[3, :, cols]
      for hh in range(2):
        keep = first if hh == 0 else jnp.logical_not(first)
        qms.append(jnp.where(keep, qt, jnp.zeros_like(qt)))
        kts.append(k_ref[rows, cols])
        qss.append(stack(qf, qb, hh).astype(BF16))
    vcols = [slice(DVP_C * h, DVP_C * (h + 1)) for h in range(H_C)]
    atts = [lax.dot_general(qms[h], kts[h], _NT, preferred_element_type=F32) for h in range(H_C)]
    cross = [jnp.dot(qss[h], u_ref[n, h].astype(BF16), preferred_element_type=F32) for h in range(H_C)]
    atts = [(atts[h] * d_ref[h]).astype(BF16) for h in range(H_C)]
    inner = [jnp.dot(atts[h], v_ref[rows, vcols[h]], preferred_element_type=F32) for h in range(H_C)]
    for h in range(H_C):
      o = inner[h] + cross[h]
      ms = jnp.sum(o * o, axis=-1, keepdims=True) * (1.0 / DV_C)
      y = o * lax.rsqrt(ms + EPS) * gn_ref[...]
      o_ref[rows, vcols[h]] = (y * g_ref[rows, vcols[h]].astype(F32)).astype(BF16)
    return carry

  lax.fori_loop(0, N, outputs, 0)


def _retention(qc, kc, vc, gc, rdq, rdh, gn):
  B, S, _ = qc.shape
  C = RET_CHUNK
  return pl.pallas_call(
      _ret_kernel,
      grid=(B,),
      in_specs=[
          pl.BlockSpec((None, S, WQ_C), lambda b: (b, 0, 0)),
          pl.BlockSpec((None, S, WQ_C), lambda b: (b, 0, 0)),
          pl.BlockSpec((None, S, WV_C), lambda b: (b, 0, 0)),
          pl.BlockSpec((None, S, WV_C), lambda b: (b, 0, 0)),
          pl.BlockSpec((2, WQ_C), lambda b: (0, 0)),
          pl.BlockSpec((2 * H_C, WQ_C), lambda b: (0, 0)),
          pl.BlockSpec((1, DVP_C), lambda b: (0, 0)),
      ],
      out_specs=pl.BlockSpec((None, S, WV_C), lambda b: (b, 0, 0)),
      out_shape=jax.ShapeDtypeStruct((B, S, WV_C), BF16),
      scratch_shapes=[
          pltpu.VMEM((S // C, H_C, 2 * DKP_C, DVP_C), F32),
          pltpu.VMEM((H_C, C, C), F32),
          pltpu.VMEM((4, C, WQ_C), F32),
      ],
      compiler_params=_cparams(1),
      name="retention",
  )(qc, kc, vc, gc, rdq, rdh, gn)


def _outproj_kernel(oa_ref, ob_ref, oc_ref, w_ref, x_ref, gate_ref, o_ref):
  y = jnp.dot(oa_ref[...], w_ref[0:W_A, :], preferred_element_type=F32)
  y += jnp.dot(ob_ref[...], w_ref[W_A:W_A + W_B, :], preferred_element_type=F32)
  y += jnp.dot(oc_ref[...], w_ref[W_A + W_B:OUT_ROWS_P, :], preferred_element_type=F32)
  o_ref[...] = x_ref[...] + gate_ref[...] * y


def _outproj(oa, ob, oc, w, x, gate):
  B, S, D = x.shape
  tm = min(ROW_TILE, S)

  def rows(width):
    return pl.BlockSpec((None, tm, width), lambda b, i: (b, i, 0))

  return pl.pallas_call(
      _outproj_kernel,
      grid=(B, S // tm),
      in_specs=[rows(W_A), rows(W_B), rows(WV_C),
                pl.BlockSpec((OUT_ROWS_P, D), lambda b, i: (0, 0)),
                rows(D),
                pl.BlockSpec((None, 1, D), lambda b, i: (b, 0, 0))],
      out_specs=rows(D),
      out_shape=jax.ShapeDtypeStruct((B, S, D), F32),
      compiler_params=_cparams(2),
      name="outproj",
  )(oa, ob, oc, w, x, gate)


def _rope_tables(S, theta, rot_dim, group):
  half = rot_dim // 2
  pos = jnp.arange(S, dtype=F32)
  inv = theta ** (-jnp.arange(0, rot_dim, 2, dtype=F32) / rot_dim)
  ang = pos[:, None] * inv[None, :]
  cos, sin = jnp.cos(ang), jnp.sin(ang)
  g = np.arange(LANES) % group
  idx = g % half
  in_first = jnp.asarray(g < half)[None, :]
  in_second = jnp.asarray((g >= half) & (g < rot_dim))[None, :]
  c = jnp.where(in_first | in_second, cos[:, idx], 1.0)
  s1 = jnp.where(in_first, -sin[:, idx], 0.0)
  s2 = jnp.where(in_second, sin[:, idx], 0.0)
  return jnp.stack([c, s1, s2]).astype(F32)


def _pad_heads(w, n_heads, d, d_pad, axis):
  shape = w.shape[:axis] + (n_heads, d) + w.shape[axis + 1:]
  pad = [(0, 0)] * (len(shape))
  pad[axis + 1] = (0, d_pad - d)
  out = jnp.pad(w.reshape(shape), pad)
  return out.reshape(w.shape[:axis] + (n_heads * d_pad,) + w.shape[axis + 1:])


def _prep_w_in(w_in):
  offs = np.concatenate([[0], np.cumsum(SPLIT_SIZES)])
  sec = [w_in[..., offs[i]:offs[i + 1]] for i in range(len(SPLIT_SIZES))]
  sec[8] = _pad_heads(sec[8], H_C, DK_C, DKP_C, w_in.ndim - 1)
  sec[9] = _pad_heads(sec[9], H_C, DK_C, DKP_C, w_in.ndim - 1)
  sec[10] = _pad_heads(sec[10], H_C, DV_C, DVP_C, w_in.ndim - 1)
  sec[11] = _pad_heads(sec[11], H_C, DV_C, DVP_C, w_in.ndim - 1)
  return jnp.concatenate(sec, axis=-1).astype(BF16)


def _prep_w_out(w_out):
  top = w_out[:, :W_A + W_B]
  bot = _pad_heads(w_out[:, W_A + W_B:], H_C, DV_C, DVP_C, 1)
  return jnp.concatenate([top, bot], axis=1).astype(BF16)


@jax.jit
def _forward(x, c, norm_g, w_ada, b_ada, w_in, w_out, qn_a, kn_a, lambda_q1, lambda_k1,
             lambda_q2, lambda_k2, subln_a, qn_b, kn_b, ret_decay, gn_c):
  B, S, D = x.shape
  L = w_in.shape[0]
  rows = 8 * ((B + 7) // 8)
  c_pad = jnp.pad(c, ((0, rows - B), (0, 0)))
  mod = _adaln(c_pad, w_ada, b_ada)[:, :B]
  w_in_p = _prep_w_in(w_in)
  w_out_p = _prep_w_out(w_out)
  taba = _rope_tables(S, ROT_THETA, ROT_A, DK_A)
  tabb = _rope_tables(S, ROT_THETA, ROT_B, DH_B)
  tabc = _rope_tables(S, RET_THETA, DK_C, DKP_C)
  gn_p = jnp.pad(gn_c, ((0, 0), (0, DVP_C - DV_C)))
  for l in range(L):
    shift = mod[l, :, None, 0:D]
    scale = mod[l, :, None, D:2 * D]
    gate = mod[l, :, None, 2 * D:3 * D]
    gains = jnp.stack([jnp.tile(qn_a[l], LANES // DK_A), jnp.tile(kn_a[l], LANES // DK_A),
                       jnp.tile(qn_b[l], LANES // DH_B), jnp.tile(kn_b[l], LANES // DH_B)])
    (qa, ka, va, ga, qb, kb, vb, gb, qc, kc, vc, gc) = _inproj(
        x, norm_g[l][None, :], shift, scale, w_in_p[l], gains, taba, tabb, tabc)
    lam_init = 0.8 - 0.6 * math.exp(-0.3 * l)
    lam_params = jnp.stack([lambda_q1[l], lambda_k1[l], lambda_q2[l], lambda_k2[l]])
    oa = _attn_a(qa, ka, va, ga, lam_params, jnp.tile(subln_a[l], H_A)[None, :], lam_init)
    ob = _attn_b(qb, kb, vb, gb)
    rdq = jnp.repeat(ret_decay[l], DKP_C, axis=1)
    rdh = jnp.broadcast_to(ret_decay[l].reshape(2 * H_C, 1), (2 * H_C, WQ_C))
    oc = _retention(qc, kc, vc, gc, rdq, rdh, gn_p[l][None, :])
    x = _outproj(oa, ob, oc, w_out_p[l], x, gate)
  return x


def kernel(x, c, norm_g, w_ada, b_ada, w_in, w_out, qn_a, kn_a, lambda_q1, lambda_k1, lambda_q2,
           lambda_k2, subln_a, qn_b, kn_b, ret_decay, gn_c):
  return _forward(x, c, norm_g, w_ada, b_ada, w_in, w_out, qn_a, kn_a, lambda_q1, lambda_k1,
                  lambda_q2, lambda_k2, subln_a, qn_b, kn_b, ret_decay, gn_c)
```

```python
import functools
import math

import numpy as np
import jax
import jax.numpy as jnp
from jax import lax
from jax.experimental import pallas as pl
from jax.experimental.pallas import tpu as pltpu

F32 = jnp.float32
BF16 = jnp.bfloat16

H_A, DK_A, DV_A = 4, 32, 64
H_B, DH_B = 6, 64
H_C, DK_C, DV_C = 4, 48, 96
SPLIT_SIZES = (256, 256, 256, 256, 384, 384, 384, 384, 192, 192, 384, 384)
ROT_THETA = 500000.0
ROT_A = DK_A // 4
ROT_B = DH_B // 4
RET_THETA = 10000.0
N_SIDE = 64
DILATIONS = (1, 4, 16)
EPS = 1e-6
NEG = -1e30

LANES = 128
MXU_DIM = 256
V7X_VMEM_BYTES = 64 * 1024 * 1024
VMEM_LIMIT = V7X_VMEM_BYTES * 7 // 8

DKP_C = 64
DVP_C = LANES
W_A = H_A * DV_A
W_B = H_B * DH_B
WQ_C = H_C * DKP_C
WV_C = H_C * DVP_C
C_QA, C_KA, C_VA, C_GA = 0, 256, 512, 768
C_QB, C_KB, C_VB, C_GB = 1024, 1408, 1792, 2176
C_QC, C_KC, C_VC, C_GC = 2560, 2816, 3072, 3584
IN_COLS_P = 4096
OUT_ROWS_P = W_A + W_B + WV_C

RET_CHUNK = 256
Q_TILE_A = 256
K_CHUNK_A = 256
LOOKAHEAD_A = 9
VX_ROWS_A = DV_A + 16
LOG2_E = math.log2(math.e)
Q_BLOCK_B = 128
K_WIN_B = Q_BLOCK_B + 2 * N_SIDE
WINDOW_OFFSETS_B = (-N_SIDE, 0, -2 * N_SIDE)
ITEMS_B = 4
ROW_TILE = 512
OUT_ROW_TILE = 1024

_NT = (((1,), (1,)), ((), ()))


def _cparams(n_grid):
  return pltpu.CompilerParams(
      dimension_semantics=("arbitrary",) * n_grid, vmem_limit_bytes=VMEM_LIMIT)


def _group_ones(group, width=LANES):
  shift = int(math.log2(group))
  r = lax.broadcasted_iota(jnp.int32, (width, width), 0) >> shift
  c = lax.broadcasted_iota(jnp.int32, (width, width), 1) >> shift
  return (r == c).astype(BF16)


def _group_sumsq(t, group):
  t2 = (t * t).astype(BF16)
  parts = []
  for c0 in range(0, t.shape[1], MXU_DIM):
    w = min(MXU_DIM, t.shape[1] - c0)
    parts.append(jnp.dot(t2[:, c0:c0 + w], _group_ones(group, w), preferred_element_type=F32))
  return parts[0] if len(parts) == 1 else jnp.concatenate(parts, axis=1)


def _rope_tile(t, tab_ref, half):
  up = pltpu.roll(t, LANES - half, 1)
  down = pltpu.roll(t, half, 1)
  return t * tab_ref[0] + up * tab_ref[1] + down * tab_ref[2]


def _silu(t):
  return t * jax.nn.sigmoid(t)


def _adaln_kernel(c_ref, w_ref, b_ref, o_ref):
  cs = _silu(c_ref[...])
  o_ref[0] = jnp.dot(cs, w_ref[0], preferred_element_type=F32,
                     precision=lax.Precision.HIGHEST) + b_ref[0]


def _adaln(c_pad, w_ada, b_ada):
  L, D, _ = w_ada.shape
  rows = c_pad.shape[0]
  return pl.pallas_call(
      _adaln_kernel,
      grid=(L, 3),
      in_specs=[
          pl.BlockSpec((rows, D), lambda l, j: (0, 0)),
          pl.BlockSpec((1, D, D), lambda l, j: (l, 0, j)),
          pl.BlockSpec((1, 1, D), lambda l, j: (l, 0, j)),
      ],
      out_specs=pl.BlockSpec((1, rows, D), lambda l, j: (l, 0, j)),
      out_shape=jax.ShapeDtypeStruct((L, rows, 3 * D), F32),
      compiler_params=_cparams(2),
      name="adaln",
  )(c_pad, w_ada, b_ada.reshape(L, 1, 3 * D))


def _inproj_kernel(x_ref, ng_ref, sh_ref, sc_ref, w_ref, gains_ref, taba_ref, tabb_ref, tabc_ref,
                   qa_ref, ka_ref, va_ref, ga_ref, qb_ref, kb_ref, vb_ref, gb_ref,
                   qc_ref, kc_ref, vc_ref, gc_ref, hb_ref):
  x = x_ref[...]
  ms = jnp.mean(x * x, axis=-1, keepdims=True)
  h = x * lax.rsqrt(ms + EPS) * ng_ref[...]
  h = h * (1.0 + sc_ref[...]) + sh_ref[...]
  hb_ref[...] = h.astype(BF16)

  def proj(c0, width):
    return jnp.dot(hb_ref[...], w_ref[:, c0:c0 + width], preferred_element_type=F32)

  def tiles(p):
    return [p[:, LANES * t:LANES * (t + 1)] for t in range(p.shape[1] // LANES)]

  gains = gains_ref[...]

  def normed_tiles(p, group, gain_row):
    y = p * lax.rsqrt(_group_sumsq(p, group) * (1.0 / group) + EPS)
    return [t * gain_row for t in tiles(y)]

  for t, p in enumerate(normed_tiles(proj(C_QA, W_A), DK_A, gains[0:1])):
    y = _rope_tile(p, taba_ref, ROT_A // 2) * (DK_A ** -0.5 * LOG2_E)
    qa_ref[LANES * t:LANES * (t + 1), :] = y.T.astype(BF16)
  for t, p in enumerate(normed_tiles(proj(C_KA, W_A), DK_A, gains[1:2])):
    y = _rope_tile(p, taba_ref, ROT_A // 2)
    ka_ref[:, LANES * t:LANES * (t + 1)] = y.astype(BF16)
  for t, p in enumerate(tiles(proj(C_VA, W_A))):
    va_ref[LANES * t:LANES * (t + 1), :] = p.T.astype(BF16)
  ga_ref[...] = _silu(proj(C_GA, W_A)).astype(BF16)

  for t, p in enumerate(normed_tiles(proj(C_QB, W_B), DH_B, gains[2:3])):
    qb_ref[t] = _rope_tile(p, tabb_ref, ROT_B // 2) * (DH_B ** -0.5 * LOG2_E)
  for t, p in enumerate(normed_tiles(proj(C_KB, W_B), DH_B, gains[3:4])):
    kb_ref[t] = _rope_tile(p, tabb_ref, ROT_B // 2)
  for t, p in enumerate(tiles(proj(C_VB, W_B))):
    vb_ref[t] = p
  gb_ref[...] = _silu(proj(C_GB, W_B)).astype(BF16)

  for t, p in enumerate(tiles(proj(C_QC, WQ_C))):
    qc_ref[:, LANES * t:LANES * (t + 1)] = _rope_tile(p, tabc_ref, DK_C // 2).astype(BF16)
  for t, p in enumerate(tiles(proj(C_KC, WQ_C))):
    y = _rope_tile(p, tabc_ref, DK_C // 2) * (DK_C ** -0.5)
    kc_ref[:, LANES * t:LANES * (t + 1)] = y.astype(BF16)
  vc_ref[...] = proj(C_VC, WV_C).astype(BF16)
  gc_ref[...] = _silu(proj(C_GC, WV_C)).astype(BF16)


def _inproj(x, ng, shift, scale, w, gains, taba, tabb, tabc):
  B, S, D = x.shape
  tm = min(ROW_TILE, S)
  n_pair = W_B // LANES

  def rows(width, dtype=BF16):
    return (pl.BlockSpec((None, tm, width), lambda i, b: (b, i, 0)),
            jax.ShapeDtypeStruct((B, S, width), dtype))

  def slabs():
    return (pl.BlockSpec((None, n_pair, tm, LANES), lambda i, b: (b, 0, i, 0)),
            jax.ShapeDtypeStruct((B, n_pair, S, LANES), F32))

  def feature_major(width):
    return (pl.BlockSpec((None, width, tm), lambda i, b: (b, 0, i)),
            jax.ShapeDtypeStruct((B, width, S), BF16))

  outs = [feature_major(W_A), rows(W_A), feature_major(W_A), rows(W_A),
          slabs(), slabs(), slabs(), rows(W_B),
          rows(WQ_C), rows(WQ_C), rows(WV_C), rows(WV_C)]
  tab_spec = pl.BlockSpec((3, tm, LANES), lambda i, b: (0, i, 0))
  return pl.pallas_call(
      _inproj_kernel,
      grid=(S // tm, B),
      in_specs=[
          pl.BlockSpec((None, tm, D), lambda i, b: (b, i, 0)),
          pl.BlockSpec((1, D), lambda i, b: (0, 0)),
          pl.BlockSpec((None, 1, D), lambda i, b: (b, 0, 0)),
          pl.BlockSpec((None, 1, D), lambda i, b: (b, 0, 0)),
          pl.BlockSpec((D, IN_COLS_P), lambda i, b: (0, 0)),
          pl.BlockSpec((4, LANES), lambda i, b: (0, 0)),
          tab_spec, tab_spec, tab_spec,
      ],
      out_specs=[o[0] for o in outs],
      out_shape=[o[1] for o in outs],
      scratch_shapes=[pltpu.VMEM((tm, D), BF16)],
      compiler_params=_cparams(2),
      name="inproj",
  )(x, ng, shift, scale, w, gains, taba, tabb, tabc)


def _attn_a_kernel(qt_ref, k_ref, vt_ref, g_ref, lam_ref, sub_ref, o_ref,
                   vx_ref, ot_ref, *, lam_init):
  S = k_ref.shape[0]
  nq = qt_ref.shape[1]
  tk = min(K_CHUNK_A, S)

  @pl.when(pl.program_id(1) == 0)
  def _():
    for h in range(H_A):
      vx_ref[h, 0:DV_A, :] = vt_ref[DV_A * h:DV_A * (h + 1), :]
      vx_ref[h, DV_A:VX_ROWS_A, :] = jnp.ones((VX_ROWS_A - DV_A, S), BF16)

  lp = lam_ref[...]
  lam = (jnp.exp(jnp.sum(lp[0:1] * lp[1:2], axis=-1, keepdims=True))
         - jnp.exp(jnp.sum(lp[2:3] * lp[3:4], axis=-1, keepdims=True)) + lam_init)
  row = lax.broadcasted_iota(jnp.int32, (W_A, 1), 0)

  qt = qt_ref[...]

  def masked_q(h, m):
    lo = h * DV_A + m * DK_A
    return jnp.where((row >= lo) & (row < lo + DK_A), qt, jnp.zeros_like(qt))

  def scores(unit):
    h, m, j = unit
    return jnp.dot(k_ref[j * tk:(j + 1) * tk, :], masked_q(h, m), preferred_element_type=F32)

  units = [(h, m, j) for h in range(H_A) for j in range(S // tk) for m in range(2)]
  pending = [scores(u) for u in units[:LOOKAHEAD_A]]
  run = {}
  for idx, (h, m, j) in enumerate(units):
    st = pending.pop(0)
    if idx + LOOKAHEAD_A < len(units):
      pending.append(scores(units[idx + LOOKAHEAD_A]))
    cm = jnp.max(st, axis=0, keepdims=True)
    vx = vx_ref[h, :, j * tk:(j + 1) * tk]
    m_new = cm if (h, m) not in run else jnp.maximum(run[h, m][0], cm)
    e = jnp.exp2(st - m_new).astype(BF16)
    pv = jnp.dot(vx, e, preferred_element_type=F32)
    if (h, m) not in run:
      run[h, m] = (m_new, pv)
    else:
      m_old, acc = run[h, m]
      run[h, m] = (m_new, jnp.exp2(m_old - m_new) * acc + pv)
    if m == 1 and j == S // tk - 1:
      maps = [run[h, mm][1] for mm in range(2)]
      maps = [acc[0:DV_A] / acc[DV_A:DV_A + 1] for acc in maps]
      ot_ref[DV_A * h:DV_A * (h + 1), :] = maps[0] - lam * maps[1]

  for t in range(W_A // LANES):
    cols = slice(LANES * t, LANES * (t + 1))
    o = ot_ref[cols, :].T
    y = o * lax.rsqrt(_group_sumsq(o, DV_A) * (1.0 / DV_A) + EPS) * sub_ref[:, cols]
    y = y * (1.0 - lam_init) * g_ref[:, cols].astype(F32)
    o_ref[:, cols] = y.astype(BF16)


def _attn_a(qat, ka, vat, ga, lam_params, sub, lam_init):
  B, S, _ = ka.shape
  tq = min(Q_TILE_A, S)
  return pl.pallas_call(
      functools.partial(_attn_a_kernel, lam_init=lam_init),
      grid=(B, S // tq),
      in_specs=[
          pl.BlockSpec((None, W_A, tq), lambda b, i: (b, 0, i)),
          pl.BlockSpec((None, S, W_A), lambda b, i: (b, 0, 0)),
          pl.BlockSpec((None, W_A, S), lambda b, i: (b, 0, 0)),
          pl.BlockSpec((None, tq, W_A), lambda b, i: (b, i, 0)),
          pl.BlockSpec((4, DK_A), lambda b, i: (0, 0)),
          pl.BlockSpec((1, W_A), lambda b, i: (0, 0)),
      ],
      out_specs=pl.BlockSpec((None, tq, W_A), lambda b, i: (b, i, 0)),
      out_shape=jax.ShapeDtypeStruct((B, S, W_A), BF16),
      scratch_shapes=[
          pltpu.VMEM((H_A, VX_ROWS_A, S), BF16),
          pltpu.VMEM((W_A, tq), F32),
      ],
      compiler_params=_cparams(2),
      name="attn_a",
  )(qat, ka, vat, ga, lam_params, sub)


def _attn_b_kernel(q_ref, k_ref, v_ref, g_ref, o_ref, m_ref, l_ref, acc_ref, bias_ref):
  S = q_ref.shape[0]
  lane = lax.broadcasted_iota(jnp.int32, (1, LANES), 1)
  first = lane < DH_B
  ones = jnp.ones((K_WIN_B, LANES), BF16)

  def band(n_q, off):
    qi = lax.broadcasted_iota(jnp.int32, (2 * n_q, K_WIN_B), 0) & (n_q - 1)
    kj = lax.broadcasted_iota(jnp.int32, (2 * n_q, K_WIN_B), 1)
    return jnp.where(jnp.abs(kj - qi + off) <= N_SIDE, 0.0, NEG).astype(F32)

  for idx, off in enumerate(WINDOW_OFFSETS_B):
    bias_ref[idx, 0:2 * Q_BLOCK_B, :] = band(Q_BLOCK_B, off)
  bias_ref[len(WINDOW_OFFSETS_B)] = band(K_WIN_B, 0)

  for gi, d in enumerate(sorted(DILATIONS, reverse=True)):
    class_len = S // d
    qblk = class_len if class_len <= K_WIN_B else Q_BLOCK_B
    n_qblk = class_len // qblk
    n_items = min(ITEMS_B * Q_BLOCK_B // qblk, d * n_qblk)
    top, bot = slice(0, qblk), slice(qblk, 2 * qblk)
    shape = (qblk, LANES)

    def items(ib, carry, d=d, gi=gi, class_len=class_len, qblk=qblk, n_qblk=n_qblk, n_items=n_items,
              top=top, bot=bot, shape=shape):
      loaded = []
      for u in range(n_items):
        i = ib * n_items + u
        r = i // n_qblk
        a0 = (i % n_qblk) * qblk
        ak = jnp.clip(a0 - N_SIDE, 0, class_len - K_WIN_B)
        if d == 1:
          q_rows = pl.ds(pl.multiple_of(a0, qblk), qblk)
          k_rows = pl.ds(pl.multiple_of(ak, N_SIDE), K_WIN_B)
        else:
          q_rows = pl.ds(d * a0 + r, qblk, stride=d)
          k_rows = pl.ds(d * ak + r, K_WIN_B, stride=d)
        qt = q_ref[q_rows, :]
        q2 = jnp.concatenate([jnp.where(first, qt, 0.0), jnp.where(first, 0.0, qt)], axis=0).astype(BF16)
        kt = k_ref[k_rows, :].astype(BF16)
        vt = jnp.concatenate([v_ref[k_rows, :].astype(BF16), ones], axis=1)
        if qblk == K_WIN_B:
          which = len(WINDOW_OFFSETS_B)
        else:
          which = jnp.where(a0 == 0, 1, jnp.where(a0 == class_len - qblk, 2, 0))
        loaded.append((q_rows, which, q2, kt, vt))
      scores = [lax.dot_general(q2, kt, _NT, preferred_element_type=F32) for _, _, q2, kt, _ in loaded]
      probs = []
      for (_, which, _, _, _), s in zip(loaded, scores):
        s = s + bias_ref[which, 0:2 * qblk, :]
        m = jnp.max(s, axis=-1, keepdims=True)
        probs.append((m, jnp.exp2(s - m).astype(BF16)))
      outs = [jnp.dot(e, vt, preferred_element_type=F32) for (_, e), (_, _, _, _, vt) in zip(probs, loaded)]
      for (q_rows, _, _, _, _), (m, _), pv in zip(loaded, probs, outs):
        m_new = jnp.where(first, jnp.broadcast_to(m[top], shape), jnp.broadcast_to(m[bot], shape))
        l_new = jnp.where(first, pv[top, LANES:], pv[bot, LANES:])
        pv_new = jnp.where(first, pv[top, :LANES], pv[bot, :LANES])
        if gi == 0:
          m_ref[q_rows, :] = m_new
          l_ref[q_rows, :] = l_new
          acc_ref[q_rows, :] = pv_new
        else:
          m_old = m_ref[q_rows, :]
          m_tot = jnp.maximum(m_old, m_new)
          w_old = jnp.exp2(m_old - m_tot)
          w_new = jnp.exp2(m_new - m_tot)
          m_ref[q_rows, :] = m_tot
          l_ref[q_rows, :] = w_old * l_ref[q_rows, :] + w_new * l_new
          acc_ref[q_rows, :] = w_old * acc_ref[q_rows, :] + w_new * pv_new
      return carry

    lax.fori_loop(0, d * n_qblk // n_items, items, 0)

  def finish(c, carry):
    rows = pl.ds(pl.multiple_of(c * ROW_TILE, ROW_TILE), ROW_TILE)
    y = acc_ref[rows, :] / l_ref[rows, :] * g_ref[rows, :].astype(F32)
    o_ref[rows, :] = y.astype(BF16)
    return carry

  lax.fori_loop(0, S // ROW_TILE, finish, 0)


def _attn_b(qb, kb, vb, gb):
  B, n_pair, S, _ = qb.shape
  slab = pl.BlockSpec((None, None, S, LANES), lambda b, j: (b, j, 0, 0))
  cols = pl.BlockSpec((None, S, LANES), lambda b, j: (b, 0, j))
  return pl.pallas_call(
      _attn_b_kernel,
      grid=(B, n_pair),
      in_specs=[slab, slab, slab, cols],
      out_specs=cols,
      out_shape=jax.ShapeDtypeStruct((B, S, W_B), BF16),
      scratch_shapes=[pltpu.VMEM((S, LANES), F32)] * 3
      + [pltpu.VMEM((len(WINDOW_OFFSETS_B) + 1, 2 * K_WIN_B, K_WIN_B), F32)],
      compiler_params=_cparams(2),
      name="attn_b",
  )(qb, kb, vb, gb)


def _log_sigmoid(t):
  return jnp.minimum(t, 0.0) - jnp.log1p(jnp.exp(-jnp.abs(t)))


def _ret_kernel(q_ref, k_ref, v_ref, g_ref, rdq_ref, rdh_ref, gn_ref, o_ref, u_ref, d_ref, tab_ref):
  S = q_ref.shape[0]
  C = RET_CHUNK
  N = S // C
  lgq = _log_sigmoid(rdq_ref[...])
  lgh = _log_sigmoid(rdh_ref[...])
  pos = lax.broadcasted_iota(jnp.int32, (C, 1), 0).astype(F32)
  tab_ref[0] = jnp.exp(lgq[0:1] * (C - 1.0 - pos))
  tab_ref[1] = jnp.exp(lgq[1:2] * pos)
  tab_ref[2] = jnp.exp(lgq[0:1] * (pos + 1.0))
  tab_ref[3] = jnp.exp(lgq[1:2] * (C - pos))
  dd = (lax.broadcasted_iota(jnp.int32, (C, C), 0) - lax.broadcasted_iota(jnp.int32, (C, C), 1))
  dist = jnp.abs(dd).astype(F32)
  for h in range(H_C):
    d_ref[h] = jnp.exp(jnp.where(dd >= 0, lgh[h:h + 1], lgh[H_C + h:H_C + h + 1]) * dist)
  cdec = jnp.exp(lgh[:, 0:LANES] * float(C))

  lane = lax.broadcasted_iota(jnp.int32, (1, LANES), 1)
  first = lane < DKP_C

  def stack(fwd, bwd, hh):
    if hh == 0:
      return jnp.where(first, fwd, pltpu.roll(bwd, DKP_C, 1))
    return jnp.where(first, pltpu.roll(fwd, DKP_C, 1), bwd)

  def chunk_rows(n):
    return pl.ds(pl.multiple_of(n * C, C), C)

  def local_state(n, carry):
    rows = chunk_rows(n)
    stacks = []
    for t in range(WQ_C // LANES):
      cols = slice(LANES * t, LANES * (t + 1))
      kt = k_ref[rows, cols].astype(F32)
      kf = kt * tab_ref[0, :, cols]
      kb = kt * tab_ref[1, :, cols]
      stacks += [stack(kf, kb, hh).T.astype(BF16) for hh in range(2)]
    states = [jnp.dot(st, v_ref[rows, DVP_C * h:DVP_C * (h + 1)], preferred_element_type=F32)
              for h, st in enumerate(stacks)]
    for h, u in enumerate(states):
      u_ref[n, h] = u
    return carry

  lax.fori_loop(0, N, local_state, 0)

  zero = tuple(jnp.zeros((DKP_C, DVP_C), F32) for _ in range(H_C))

  def scan_fwd(n, state):
    new = []
    for h in range(H_C):
      u = u_ref[n, h, 0:DKP_C, :]
      u_ref[n, h, 0:DKP_C, :] = state[h]
      new.append(state[h] * cdec[h:h + 1] + u)
    return tuple(new)

  lax.fori_loop(0, N, scan_fwd, zero)

  def scan_bwd(i, state):
    n = N - 1 - i
    new = []
    for h in range(H_C):
      u = u_ref[n, h, DKP_C:2 * DKP_C, :]
      u_ref[n, h, DKP_C:2 * DKP_C, :] = state[h]
      new.append(state[h] * cdec[H_C + h:H_C + h + 1] + u)
    return tuple(new)

  lax.fori_loop(0, N, scan_bwd, zero)

  def outputs(n, carry):
    rows = chunk_rows(n)
    qms, kts, qss = [], [], []
    for t in range(WQ_C // LANES):
      cols = slice(LANES * t, LANES * (t + 1))
      qt = q_ref[rows, cols]
      q32 = qt.astype(F32)
      qf = q32 * tab_ref[2, :, cols]
      qb = q32 * tab_ref[3, :, cols]
      for hh in range(2):
        keep = first if hh == 0 else jnp.logical_not(first)
        qms.append(jnp.where(keep, qt, jnp.zeros_like(qt)))
        kts.append(k_ref[rows, cols])
        qss.append(stack(qf, qb, hh).astype(BF16))
    vcols = [slice(DVP_C * h, DVP_C * (h + 1)) for h in range(H_C)]
    atts = [lax.dot_general(qms[h], kts[h], _NT, preferred_element_type=F32) for h in range(H_C)]
    cross = [jnp.dot(qss[h], u_ref[n, h].astype(BF16), preferred_element_type=F32) for h in range(H_C)]
    atts = [(atts[h] * d_ref[h]).astype(BF16) for h in range(H_C)]
    inner = [jnp.dot(atts[h], v_ref[rows, vcols[h]], preferred_element_type=F32) for h in range(H_C)]
    for h in range(H_C):
      o = inner[h] + cross[h]
      ms = jnp.sum(o * o, axis=-1, keepdims=True) * (1.0 / DV_C)
      y = o * lax.rsqrt(ms + EPS) * gn_ref[...]
      o_ref[rows, vcols[h]] = (y * g_ref[rows, vcols[h]].astype(F32)).astype(BF16)
    return carry

  lax.fori_loop(0, N, outputs, 0)


def _retention(qc, kc, vc, gc, rdq, rdh, gn):
  B, S, _ = qc.shape
  C = RET_CHUNK
  return pl.pallas_call(
      _ret_kernel,
      grid=(B,),
      in_specs=[
          pl.BlockSpec((None, S, WQ_C), lambda b: (b, 0, 0)),
          pl.BlockSpec((None, S, WQ_C), lambda b: (b, 0, 0)),
          pl.BlockSpec((None, S, WV_C), lambda b: (b, 0, 0)),
          pl.BlockSpec((None, S, WV_C), lambda b: (b, 0, 0)),
          pl.BlockSpec((2, WQ_C), lambda b: (0, 0)),
          pl.BlockSpec((2 * H_C, WQ_C), lambda b: (0, 0)),
          pl.BlockSpec((1, DVP_C), lambda b: (0, 0)),
      ],
      out_specs=pl.BlockSpec((None, S, WV_C), lambda b: (b, 0, 0)),
      out_shape=jax.ShapeDtypeStruct((B, S, WV_C), BF16),
      scratch_shapes=[
          pltpu.VMEM((S // C, H_C, 2 * DKP_C, DVP_C), F32),
          pltpu.VMEM((H_C, C, C), F32),
          pltpu.VMEM((4, C, WQ_C), F32),
      ],
      compiler_params=_cparams(1),
      name="retention",
  )(qc, kc, vc, gc, rdq, rdh, gn)


def _outproj_kernel(oa_ref, ob_ref, oc_ref, w_ref, x_ref, gate_ref, o_ref):
  y = jnp.dot(oa_ref[...], w_ref[0:W_A, :], preferred_element_type=F32)
  y += jnp.dot(ob_ref[...], w_ref[W_A:W_A + W_B, :], preferred_element_type=F32)
  y += jnp.dot(oc_ref[...], w_ref[W_A + W_B:OUT_ROWS_P, :], preferred_element_type=F32)
  o_ref[...] = x_ref[...] + gate_ref[...] * y


def _outproj(oa, ob, oc, w, x, gate):
  B, S, D = x.shape
  tm = min(OUT_ROW_TILE, S)

  def rows(width):
    return pl.BlockSpec((None, tm, width), lambda b, i: (b, i, 0))

  return pl.pallas_call(
      _outproj_kernel,
      grid=(B, S // tm),
      in_specs=[rows(W_A), rows(W_B), rows(WV_C),
                pl.BlockSpec((OUT_ROWS_P, D), lambda b, i: (0, 0)),
                rows(D),
                pl.BlockSpec((None, 1, D), lambda b, i: (b, 0, 0))],
      out_specs=rows(D),
      out_shape=jax.ShapeDtypeStruct((B, S, D), F32),
      compiler_params=_cparams(2),
      name="outproj",
  )(oa, ob, oc, w, x, gate)


def _rope_tables(S, theta, rot_dim, group):
  half = rot_dim // 2
  pos = jnp.arange(S, dtype=F32)
  inv = theta ** (-jnp.arange(0, rot_dim, 2, dtype=F32) / rot_dim)
  ang = pos[:, None] * inv[None, :]
  cos, sin = jnp.cos(ang), jnp.sin(ang)
  g = np.arange(LANES) % group
  idx = g % half
  in_first = jnp.asarray(g < half)[None, :]
  in_second = jnp.asarray((g >= half) & (g < rot_dim))[None, :]
  c = jnp.where(in_first | in_second, cos[:, idx], 1.0)
  s1 = jnp.where(in_first, -sin[:, idx], 0.0)
  s2 = jnp.where(in_second, sin[:, idx], 0.0)
  return jnp.stack([c, s1, s2]).astype(F32)


def _pad_heads(w, n_heads, d, d_pad, axis):
  shape = w.shape[:axis] + (n_heads, d) + w.shape[axis + 1:]
  pad = [(0, 0)] * (len(shape))
  pad[axis + 1] = (0, d_pad - d)
  out = jnp.pad(w.reshape(shape), pad)
  return out.reshape(w.shape[:axis] + (n_heads * d_pad,) + w.shape[axis + 1:])


def _prep_w_in(w_in):
  offs = np.concatenate([[0], np.cumsum(SPLIT_SIZES)])
  w_in = w_in.astype(BF16)
  sec = [w_in[..., offs[i]:offs[i + 1]] for i in range(len(SPLIT_SIZES))]
  sec[8] = _pad_heads(sec[8], H_C, DK_C, DKP_C, w_in.ndim - 1)
  sec[9] = _pad_heads(sec[9], H_C, DK_C, DKP_C, w_in.ndim - 1)
  sec[10] = _pad_heads(sec[10], H_C, DV_C, DVP_C, w_in.ndim - 1)
  sec[11] = _pad_heads(sec[11], H_C, DV_C, DVP_C, w_in.ndim - 1)
  return jnp.concatenate(sec, axis=-1)


def _prep_w_out(w_out):
  w_out = w_out.astype(BF16)
  top = w_out[:, :W_A + W_B]
  bot = _pad_heads(w_out[:, W_A + W_B:], H_C, DV_C, DVP_C, 1)
  return jnp.concatenate([top, bot], axis=1)


@jax.jit
def _forward(x, c, norm_g, w_ada, b_ada, w_in, w_out, qn_a, kn_a, lambda_q1, lambda_k1,
             lambda_q2, lambda_k2, subln_a, qn_b, kn_b, ret_decay, gn_c):
  B, S, D = x.shape
  L = w_in.shape[0]
  rows = 8 * ((B + 7) // 8)
  c_pad = jnp.pad(c, ((0, rows - B), (0, 0)))
  mod = _adaln(c_pad, w_ada, b_ada)[:, :B]
  w_in_p = _prep_w_in(w_in)
  w_out_p = _prep_w_out(w_out)
  taba = _rope_tables(S, ROT_THETA, ROT_A, DK_A)
  tabb = _rope_tables(S, ROT_THETA, ROT_B, DH_B)
  tabc = _rope_tables(S, RET_THETA, DK_C, DKP_C)
  gn_p = jnp.pad(gn_c, ((0, 0), (0, DVP_C - DV_C)))
  for l in range(L):
    shift = mod[l, :, None, 0:D]
    scale = mod[l, :, None, D:2 * D]
    gate = mod[l, :, None, 2 * D:3 * D]
    gains = jnp.stack([jnp.tile(qn_a[l], LANES // DK_A), jnp.tile(kn_a[l], LANES // DK_A),
                       jnp.tile(qn_b[l], LANES // DH_B), jnp.tile(kn_b[l], LANES // DH_B)])
    (qa, ka, va, ga, qb, kb, vb, gb, qc, kc, vc, gc) = _inproj(
        x, norm_g[l][None, :], shift, scale, w_in_p[l], gains, taba, tabb, tabc)
    lam_init = 0.8 - 0.6 * math.exp(-0.3 * l)
    lam_params = jnp.stack([lambda_q1[l], lambda_k1[l], lambda_q2[l], lambda_k2[l]])
    oa = _attn_a(qa, ka, va, ga, lam_params, jnp.tile(subln_a[l], H_A)[None, :], lam_init)
    ob = _attn_b(qb, kb, vb, gb)
    rdq = jnp.repeat(ret_decay[l], DKP_C, axis=1)
    rdh = jnp.broadcast_to(ret_decay[l].reshape(2 * H_C, 1), (2 * H_C, WQ_C))
    oc = _retention(qc, kc, vc, gc, rdq, rdh, gn_p[l][None, :])
    x = _outproj(oa, ob, oc, w_out_p[l], x, gate)
  return x


def kernel(x, c, norm_g, w_ada, b_ada, w_in, w_out, qn_a, kn_a, lambda_q1, lambda_k1, lambda_q2,
           lambda_k2, subln_a, qn_b, kn_b, ret_decay, gn_c):
  return _forward(x, c, norm_g, w_ada, b_ada, w_in, w_out, qn_a, kn_a, lambda_q1, lambda_k1,
                  lambda_q2, lambda_k2, subln_a, qn_b, kn_b, ret_decay, gn_c)
```

```python
import functools
import math

import numpy as np
import jax
import jax.numpy as jnp
from jax import lax
from jax.experimental import pallas as pl
from jax.experimental.pallas import tpu as pltpu

F32 = jnp.float32
BF16 = jnp.bfloat16

H_A, DK_A, DV_A = 4, 32, 64
H_B, DH_B = 6, 64
H_C, DK_C, DV_C = 4, 48, 96
SPLIT_SIZES = (256, 256, 256, 256, 384, 384, 384, 384, 192, 192, 384, 384)
ROT_THETA = 500000.0
ROT_A = DK_A // 4
ROT_B = DH_B // 4
RET_THETA = 10000.0
N_SIDE = 64
DILATIONS = (1, 4, 16)
EPS = 1e-6
NEG = -1e30

LANES = 128
MXU_DIM = 256
V7X_VMEM_BYTES = 64 * 1024 * 1024
VMEM_LIMIT = V7X_VMEM_BYTES * 7 // 8

DKP_C = 64
DVP_C = LANES
W_A = H_A * DV_A
W_B = H_B * DH_B
WQ_C = H_C * DKP_C
WV_C = H_C * DVP_C
C_QA, C_KA, C_VA, C_GA = 0, 256, 512, 768
C_QB, C_KB, C_VB, C_GB = 1024, 1408, 1792, 2176
C_QC, C_KC, C_VC, C_GC = 2560, 2816, 3072, 3584
IN_COLS_P = 4096
OUT_ROWS_P = W_A + W_B + WV_C

RET_CHUNK = 256
Q_TILE_A = 256
K_CHUNK_A = 256
LOOKAHEAD_A = 9
VX_ROWS_A = DV_A + 16
LOG2_E = math.log2(math.e)
Q_BLOCK_B = 128
K_WIN_B = Q_BLOCK_B + 2 * N_SIDE
WINDOW_OFFSETS_B = (-N_SIDE, 0, -2 * N_SIDE)
ITEMS_B = 4
ROW_TILE = 512
OUT_ROW_TILE = 1024

_NT = (((1,), (1,)), ((), ()))


def _cparams(n_grid):
  return pltpu.CompilerParams(
      dimension_semantics=("arbitrary",) * n_grid, vmem_limit_bytes=VMEM_LIMIT)


def _group_ones(group, width=LANES):
  shift = int(math.log2(group))
  r = lax.broadcasted_iota(jnp.int32, (width, width), 0) >> shift
  c = lax.broadcasted_iota(jnp.int32, (width, width), 1) >> shift
  return (r == c).astype(BF16)


def _group_sumsq(t, group):
  t2 = (t * t).astype(BF16)
  parts = []
  for c0 in range(0, t.shape[1], MXU_DIM):
    w = min(MXU_DIM, t.shape[1] - c0)
    parts.append(jnp.dot(t2[:, c0:c0 + w], _group_ones(group, w), preferred_element_type=F32))
  return parts[0] if len(parts) == 1 else jnp.concatenate(parts, axis=1)


def _rope_tile(t, tab_ref, half):
  up = pltpu.roll(t, LANES - half, 1)
  down = pltpu.roll(t, half, 1)
  return t * tab_ref[0] + up * tab_ref[1] + down * tab_ref[2]


def _silu(t):
  return t * jax.nn.sigmoid(t)


def _adaln_kernel(c_ref, w_ref, b_ref, o_ref):
  cs = _silu(c_ref[...])
  o_ref[0] = jnp.dot(cs, w_ref[0], preferred_element_type=F32,
                     precision=lax.Precision.HIGHEST) + b_ref[0]


def _adaln(c_pad, w_ada, b_ada):
  L, D, _ = w_ada.shape
  rows = c_pad.shape[0]
  return pl.pallas_call(
      _adaln_kernel,
      grid=(L, 3),
      in_specs=[
          pl.BlockSpec((rows, D), lambda l, j: (0, 0)),
          pl.BlockSpec((1, D, D), lambda l, j: (l, 0, j)),
          pl.BlockSpec((1, 1, D), lambda l, j: (l, 0, j)),
      ],
      out_specs=pl.BlockSpec((1, rows, D), lambda l, j: (l, 0, j)),
      out_shape=jax.ShapeDtypeStruct((L, rows, 3 * D), F32),
      compiler_params=_cparams(2),
      name="adaln",
  )(c_pad, w_ada, b_ada.reshape(L, 1, 3 * D))


def _inproj_kernel(x_ref, ng_ref, sh_ref, sc_ref, w_ref, gains_ref, taba_ref, tabb_ref, tabc_ref,
                   qa_ref, ka_ref, va_ref, ga_ref, qb_ref, kb_ref, vb_ref, gb_ref,
                   qc_ref, kc_ref, vc_ref, gc_ref, hb_ref):
  x = x_ref[...]
  ms = jnp.mean(x * x, axis=-1, keepdims=True)
  h = x * lax.rsqrt(ms + EPS) * ng_ref[...]
  h = h * (1.0 + sc_ref[...]) + sh_ref[...]
  hb_ref[...] = h.astype(BF16)

  def proj(c0, width):
    return jnp.dot(hb_ref[...], w_ref[:, c0:c0 + width], preferred_element_type=F32)

  def tiles(p):
    return [p[:, LANES * t:LANES * (t + 1)] for t in range(p.shape[1] // LANES)]

  gains = gains_ref[...]

  def normed_tiles(p, group, gain_row):
    y = p * lax.rsqrt(_group_sumsq(p, group) * (1.0 / group) + EPS)
    return [t * gain_row for t in tiles(y)]

  for t, p in enumerate(normed_tiles(proj(C_QA, W_A), DK_A, gains[0:1])):
    y = _rope_tile(p, taba_ref, ROT_A // 2) * (DK_A ** -0.5 * LOG2_E)
    qa_ref[LANES * t:LANES * (t + 1), :] = y.T.astype(BF16)
  for t, p in enumerate(normed_tiles(proj(C_KA, W_A), DK_A, gains[1:2])):
    y = _rope_tile(p, taba_ref, ROT_A // 2)
    ka_ref[:, LANES * t:LANES * (t + 1)] = y.astype(BF16)
  for t, p in enumerate(tiles(proj(C_VA, W_A))):
    va_ref[LANES * t:LANES * (t + 1), :] = p.T.astype(BF16)
  ga_ref[...] = _silu(proj(C_GA, W_A)).astype(BF16)

  for t, p in enumerate(normed_tiles(proj(C_QB, W_B), DH_B, gains[2:3])):
    qb_ref[t] = _rope_tile(p, tabb_ref, ROT_B // 2) * (DH_B ** -0.5 * LOG2_E)
  for t, p in enumerate(normed_tiles(proj(C_KB, W_B), DH_B, gains[3:4])):
    kb_ref[t] = _rope_tile(p, tabb_ref, ROT_B // 2)
  for t, p in enumerate(tiles(proj(C_VB, W_B))):
    vb_ref[t] = p
  gb_ref[...] = _silu(proj(C_GB, W_B)).astype(BF16)

  for t, p in enumerate(tiles(proj(C_QC, WQ_C))):
    qc_ref[:, LANES * t:LANES * (t + 1)] = _rope_tile(p, tabc_ref, DK_C // 2).astype(BF16)
  for t, p in enumerate(tiles(proj(C_KC, WQ_C))):
    y = _rope_tile(p, tabc_ref, DK_C // 2) * (DK_C ** -0.5)
    kc_ref[LANES * t:LANES * (t + 1), :] = y.T.astype(BF16)
  vc_ref[...] = proj(C_VC, WV_C).astype(BF16)
  gc_ref[...] = _silu(proj(C_GC, WV_C)).astype(BF16)


def _inproj(x, ng, shift, scale, w, gains, taba, tabb, tabc):
  B, S, D = x.shape
  tm = min(ROW_TILE, S)
  n_pair = W_B // LANES

  def rows(width, dtype=BF16):
    return (pl.BlockSpec((None, tm, width), lambda i, b: (b, i, 0)),
            jax.ShapeDtypeStruct((B, S, width), dtype))

  def slabs():
    return (pl.BlockSpec((None, n_pair, tm, LANES), lambda i, b: (b, 0, i, 0)),
            jax.ShapeDtypeStruct((B, n_pair, S, LANES), F32))

  def feature_major(width):
    return (pl.BlockSpec((None, width, tm), lambda i, b: (b, 0, i)),
            jax.ShapeDtypeStruct((B, width, S), BF16))

  outs = [feature_major(W_A), rows(W_A), feature_major(W_A), rows(W_A),
          slabs(), slabs(), slabs(), rows(W_B),
          rows(WQ_C), feature_major(WQ_C), rows(WV_C), rows(WV_C)]
  tab_spec = pl.BlockSpec((3, tm, LANES), lambda i, b: (0, i, 0))
  return pl.pallas_call(
      _inproj_kernel,
      grid=(S // tm, B),
      in_specs=[
          pl.BlockSpec((None, tm, D), lambda i, b: (b, i, 0)),
          pl.BlockSpec((1, D), lambda i, b: (0, 0)),
          pl.BlockSpec((None, 1, D), lambda i, b: (b, 0, 0)),
          pl.BlockSpec((None, 1, D), lambda i, b: (b, 0, 0)),
          pl.BlockSpec((D, IN_COLS_P), lambda i, b: (0, 0)),
          pl.BlockSpec((4, LANES), lambda i, b: (0, 0)),
          tab_spec, tab_spec, tab_spec,
      ],
      out_specs=[o[0] for o in outs],
      out_shape=[o[1] for o in outs],
      scratch_shapes=[pltpu.VMEM((tm, D), BF16)],
      compiler_params=_cparams(2),
      name="inproj",
  )(x, ng, shift, scale, w, gains, taba, tabb, tabc)


def _attn_a_kernel(qt_ref, k_ref, vt_ref, g_ref, lam_ref, sub_ref, o_ref,
                   vx_ref, ot_ref, *, lam_init):
  S = k_ref.shape[0]
  nq = qt_ref.shape[1]
  tk = min(K_CHUNK_A, S)

  @pl.when(pl.program_id(1) == 0)
  def _():
    for h in range(H_A):
      vx_ref[h, 0:DV_A, :] = vt_ref[DV_A * h:DV_A * (h + 1), :]
      vx_ref[h, DV_A:VX_ROWS_A, :] = jnp.ones((VX_ROWS_A - DV_A, S), BF16)

  lp = lam_ref[...]
  lam = (jnp.exp(jnp.sum(lp[0:1] * lp[1:2], axis=-1, keepdims=True))
         - jnp.exp(jnp.sum(lp[2:3] * lp[3:4], axis=-1, keepdims=True)) + lam_init)
  row = lax.broadcasted_iota(jnp.int32, (W_A, 1), 0)

  qt = qt_ref[...]

  def masked_q(h, m):
    lo = h * DV_A + m * DK_A
    return jnp.where((row >= lo) & (row < lo + DK_A), qt, jnp.zeros_like(qt))

  def scores(unit):
    h, m, j = unit
    return jnp.dot(k_ref[j * tk:(j + 1) * tk, :], masked_q(h, m), preferred_element_type=F32)

  units = [(h, m, j) for h in range(H_A) for j in range(S // tk) for m in range(2)]
  pending = [scores(u) for u in units[:LOOKAHEAD_A]]
  run = {}
  for idx, (h, m, j) in enumerate(units):
    st = pending.pop(0)
    if idx + LOOKAHEAD_A < len(units):
      pending.append(scores(units[idx + LOOKAHEAD_A]))
    cm = jnp.max(st, axis=0, keepdims=True)
    vx = vx_ref[h, :, j * tk:(j + 1) * tk]
    m_new = cm if (h, m) not in run else jnp.maximum(run[h, m][0], cm)
    e = jnp.exp2(st - m_new).astype(BF16)
    pv = jnp.dot(vx, e, preferred_element_type=F32)
    if (h, m) not in run:
      run[h, m] = (m_new, pv)
    else:
      m_old, acc = run[h, m]
      run[h, m] = (m_new, jnp.exp2(m_old - m_new) * acc + pv)
    if m == 1 and j == S // tk - 1:
      maps = [run[h, mm][1] for mm in range(2)]
      maps = [acc[0:DV_A] / acc[DV_A:DV_A + 1] for acc in maps]
      ot_ref[DV_A * h:DV_A * (h + 1), :] = maps[0] - lam * maps[1]

  for t in range(W_A // LANES):
    cols = slice(LANES * t, LANES * (t + 1))
    o = ot_ref[cols, :].T
    y = o * lax.rsqrt(_group_sumsq(o, DV_A) * (1.0 / DV_A) + EPS) * sub_ref[:, cols]
    y = y * (1.0 - lam_init) * g_ref[:, cols].astype(F32)
    o_ref[:, cols] = y.astype(BF16)


def _attn_a(qat, ka, vat, ga, lam_params, sub, lam_init):
  B, S, _ = ka.shape
  tq = min(Q_TILE_A, S)
  return pl.pallas_call(
      functools.partial(_attn_a_kernel, lam_init=lam_init),
      grid=(B, S // tq),
      in_specs=[
          pl.BlockSpec((None, W_A, tq), lambda b, i: (b, 0, i)),
          pl.BlockSpec((None, S, W_A), lambda b, i: (b, 0, 0)),
          pl.BlockSpec((None, W_A, S), lambda b, i: (b, 0, 0)),
          pl.BlockSpec((None, tq, W_A), lambda b, i: (b, i, 0)),
          pl.BlockSpec((4, DK_A), lambda b, i: (0, 0)),
          pl.BlockSpec((1, W_A), lambda b, i: (0, 0)),
      ],
      out_specs=pl.BlockSpec((None, tq, W_A), lambda b, i: (b, i, 0)),
      out_shape=jax.ShapeDtypeStruct((B, S, W_A), BF16),
      scratch_shapes=[
          pltpu.VMEM((H_A, VX_ROWS_A, S), BF16),
          pltpu.VMEM((W_A, tq), F32),
      ],
      compiler_params=_cparams(2),
      name="attn_a",
  )(qat, ka, vat, ga, lam_params, sub)


def _attn_b_kernel(q_ref, k_ref, v_ref, g_ref, o_ref, m_ref, l_ref, acc_ref, bias_ref):
  S = q_ref.shape[0]
  lane = lax.broadcasted_iota(jnp.int32, (1, LANES), 1)
  first = lane < DH_B
  ones = jnp.ones((K_WIN_B, LANES), BF16)

  def band(n_q, off):
    qi = lax.broadcasted_iota(jnp.int32, (2 * n_q, K_WIN_B), 0) & (n_q - 1)
    kj = lax.broadcasted_iota(jnp.int32, (2 * n_q, K_WIN_B), 1)
    return jnp.where(jnp.abs(kj - qi + off) <= N_SIDE, 0.0, NEG).astype(F32)

  for idx, off in enumerate(WINDOW_OFFSETS_B):
    bias_ref[idx, 0:2 * Q_BLOCK_B, :] = band(Q_BLOCK_B, off)
  bias_ref[len(WINDOW_OFFSETS_B)] = band(K_WIN_B, 0)

  for gi, d in enumerate(sorted(DILATIONS, reverse=True)):
    class_len = S // d
    qblk = class_len if class_len <= K_WIN_B else Q_BLOCK_B
    n_qblk = class_len // qblk
    n_items = min(ITEMS_B * Q_BLOCK_B // qblk, d * n_qblk)
    top, bot = slice(0, qblk), slice(qblk, 2 * qblk)
    shape = (qblk, LANES)

    def items(ib, carry, d=d, gi=gi, class_len=class_len, qblk=qblk, n_qblk=n_qblk, n_items=n_items,
              top=top, bot=bot, shape=shape):
      loaded = []
      for u in range(n_items):
        i = ib * n_items + u
        r = i // n_qblk
        a0 = (i % n_qblk) * qblk
        ak = jnp.clip(a0 - N_SIDE, 0, class_len - K_WIN_B)
        if d == 1:
          q_rows = pl.ds(pl.multiple_of(a0, qblk), qblk)
          k_rows = pl.ds(pl.multiple_of(ak, N_SIDE), K_WIN_B)
        else:
          q_rows = pl.ds(d * a0 + r, qblk, stride=d)
          k_rows = pl.ds(d * ak + r, K_WIN_B, stride=d)
        qt = q_ref[q_rows, :]
        q2 = jnp.concatenate([jnp.where(first, qt, 0.0), jnp.where(first, 0.0, qt)], axis=0).astype(BF16)
        kt = k_ref[k_rows, :].astype(BF16)
        vt = jnp.concatenate([v_ref[k_rows, :].astype(BF16), ones], axis=1)
        if qblk == K_WIN_B:
          which = len(WINDOW_OFFSETS_B)
        else:
          which = jnp.where(a0 == 0, 1, jnp.where(a0 == class_len - qblk, 2, 0))
        loaded.append((q_rows, which, q2, kt, vt))
      scores = [lax.dot_general(q2, kt, _NT, preferred_element_type=F32) for _, _, q2, kt, _ in loaded]
      probs = []
      for (_, which, _, _, _), s in zip(loaded, scores):
        s = s + bias_ref[which, 0:2 * qblk, :]
        m = jnp.max(s, axis=-1, keepdims=True)
        probs.append((m, jnp.exp2(s - m).astype(BF16)))
      outs = [jnp.dot(e, vt, preferred_element_type=F32) for (_, e), (_, _, _, _, vt) in zip(probs, loaded)]
      for (q_rows, _, _, _, _), (m, _), pv in zip(loaded, probs, outs):
        m_new = jnp.where(first, jnp.broadcast_to(m[top], shape), jnp.broadcast_to(m[bot], shape))
        l_new = jnp.where(first, pv[top, LANES:], pv[bot, LANES:])
        pv_new = jnp.where(first, pv[top, :LANES], pv[bot, :LANES])
        if gi == 0:
          m_ref[q_rows, :] = m_new
          l_ref[q_rows, :] = l_new
          acc_ref[q_rows, :] = pv_new
        else:
          m_old = m_ref[q_rows, :]
          m_tot = jnp.maximum(m_old, m_new)
          w_old = jnp.exp2(m_old - m_tot)
          w_new = jnp.exp2(m_new - m_tot)
          m_ref[q_rows, :] = m_tot
          l_ref[q_rows, :] = w_old * l_ref[q_rows, :] + w_new * l_new
          acc_ref[q_rows, :] = w_old * acc_ref[q_rows, :] + w_new * pv_new
      return carry

    lax.fori_loop(0, d * n_qblk // n_items, items, 0)

  def finish(c, carry):
    rows = pl.ds(pl.multiple_of(c * ROW_TILE, ROW_TILE), ROW_TILE)
    y = acc_ref[rows, :] / l_ref[rows, :] * g_ref[rows, :].astype(F32)
    o_ref[rows, :] = y.astype(BF16)
    return carry

  lax.fori_loop(0, S // ROW_TILE, finish, 0)


def _attn_b(qb, kb, vb, gb):
  B, n_pair, S, _ = qb.shape
  slab = pl.BlockSpec((None, None, S, LANES), lambda b, j: (b, j, 0, 0))
  cols = pl.BlockSpec((None, S, LANES), lambda b, j: (b, 0, j))
  return pl.pallas_call(
      _attn_b_kernel,
      grid=(B, n_pair),
      in_specs=[slab, slab, slab, cols],
      out_specs=cols,
      out_shape=jax.ShapeDtypeStruct((B, S, W_B), BF16),
      scratch_shapes=[pltpu.VMEM((S, LANES), F32)] * 3
      + [pltpu.VMEM((len(WINDOW_OFFSETS_B) + 1, 2 * K_WIN_B, K_WIN_B), F32)],
      compiler_params=_cparams(2),
      name="attn_b",
  )(qb, kb, vb, gb)


def _log_sigmoid(t):
  return jnp.minimum(t, 0.0) - jnp.log1p(jnp.exp(-jnp.abs(t)))


def _ret_kernel(q_ref, kt_ref, v_ref, g_ref, rdq_ref, rdh_ref, gn_ref, o_ref, u_ref, d_ref, tab_ref):
  S = q_ref.shape[0]
  C = RET_CHUNK
  N = S // C
  lgq = _log_sigmoid(rdq_ref[...])
  lgh = _log_sigmoid(rdh_ref[...])
  pos = lax.broadcasted_iota(jnp.int32, (C, 1), 0).astype(F32)
  tab_ref[0] = jnp.exp(lgq[0:1] * (pos + 1.0))
  tab_ref[1] = jnp.exp(lgq[1:2] * (C - pos))
  pos_row = lax.broadcasted_iota(jnp.int32, (1, C), 1).astype(F32)
  kdec = jnp.concatenate([jnp.exp(lgh[0:H_C] * (C - 1.0 - pos_row)), jnp.exp(lgh[H_C:] * pos_row)], axis=0)
  dd = (lax.broadcasted_iota(jnp.int32, (C, C), 0) - lax.broadcasted_iota(jnp.int32, (C, C), 1))
  dist = jnp.abs(dd).astype(F32)
  for h in range(H_C):
    d_ref[h] = jnp.exp(jnp.where(dd >= 0, lgh[h:h + 1], lgh[H_C + h:H_C + h + 1]) * dist)
  cdec = jnp.exp(lgh[:, 0:LANES] * float(C))

  lane = lax.broadcasted_iota(jnp.int32, (1, LANES), 1)
  first = lane < DKP_C

  def stack(fwd, bwd, hh):
    if hh == 0:
      return jnp.where(first, fwd, pltpu.roll(bwd, DKP_C, 1))
    return jnp.where(first, pltpu.roll(fwd, DKP_C, 1), bwd)

  def chunk_rows(n):
    return pl.ds(pl.multiple_of(n * C, C), C)

  def local_state(n, carry):
    rows = chunk_rows(n)
    stacks = []
    for h in range(H_C):
      kt = kt_ref[DKP_C * h:DKP_C * (h + 1), rows].astype(F32)
      stacks.append(jnp.concatenate([kt * kdec[h:h + 1], kt * kdec[H_C + h:H_C + h + 1]],
                                    axis=0).astype(BF16))
    states = [jnp.dot(st, v_ref[rows, DVP_C * h:DVP_C * (h + 1)], preferred_element_type=F32)
              for h, st in enumerate(stacks)]
    for h, u in enumerate(states):
      u_ref[n, h] = u
    return carry

  lax.fori_loop(0, N, local_state, 0)

  zero = tuple(jnp.zeros((DKP_C, DVP_C), F32) for _ in range(H_C))

  def scan_fwd(n, state):
    new = []
    for h in range(H_C):
      u = u_ref[n, h, 0:DKP_C, :]
      u_ref[n, h, 0:DKP_C, :] = state[h]
      new.append(state[h] * cdec[h:h + 1] + u)
    return tuple(new)

  lax.fori_loop(0, N, scan_fwd, zero)

  def scan_bwd(i, state):
    n = N - 1 - i
    new = []
    for h in range(H_C):
      u = u_ref[n, h, DKP_C:2 * DKP_C, :]
      u_ref[n, h, DKP_C:2 * DKP_C, :] = state[h]
      new.append(state[h] * cdec[H_C + h:H_C + h + 1] + u)
    return tuple(new)

  lax.fori_loop(0, N, scan_bwd, zero)

  def outputs(n, carry):
    rows = chunk_rows(n)
    qms, kts, qss = [], [], []
    for t in range(WQ_C // LANES):
      cols = slice(LANES * t, LANES * (t + 1))
      qt = q_ref[rows, cols]
      q32 = qt.astype(F32)
      qf = q32 * tab_ref[0, :, cols]
      qb = q32 * tab_ref[1, :, cols]
      for hh in range(2):
        keep = first if hh == 0 else jnp.logical_not(first)
        qms.append(jnp.where(keep, qt, jnp.zeros_like(qt)))
        kts.append(kt_ref[cols, rows])
        qss.append(stack(qf, qb, hh).astype(BF16))
    vcols = [slice(DVP_C * h, DVP_C * (h + 1)) for h in range(H_C)]
    atts = [jnp.dot(qms[h], kts[h], preferred_element_type=F32) for h in range(H_C)]
    cross = [jnp.dot(qss[h], u_ref[n, h].astype(BF16), preferred_element_type=F32) for h in range(H_C)]
    atts = [(atts[h] * d_ref[h]).astype(BF16) for h in range(H_C)]
    inner = [jnp.dot(atts[h], v_ref[rows, vcols[h]], preferred_element_type=F32) for h in range(H_C)]
    for h in range(H_C):
      o = inner[h] + cross[h]
      ms = jnp.sum(o * o, axis=-1, keepdims=True) * (1.0 / DV_C)
      y = o * lax.rsqrt(ms + EPS) * gn_ref[...]
      o_ref[rows, vcols[h]] = (y * g_ref[rows, vcols[h]].astype(F32)).astype(BF16)
    return carry

  lax.fori_loop(0, N, outputs, 0)


def _retention(qc, kct, vc, gc, rdq, rdh, gn):
  B, S, _ = qc.shape
  C = RET_CHUNK
  return pl.pallas_call(
      _ret_kernel,
      grid=(B,),
      in_specs=[
          pl.BlockSpec((None, S, WQ_C), lambda b: (b, 0, 0)),
          pl.BlockSpec((None, WQ_C, S), lambda b: (b, 0, 0)),
          pl.BlockSpec((None, S, WV_C), lambda b: (b, 0, 0)),
          pl.BlockSpec((None, S, WV_C), lambda b: (b, 0, 0)),
          pl.BlockSpec((2, WQ_C), lambda b: (0, 0)),
          pl.BlockSpec((2 * H_C, C), lambda b: (0, 0)),
          pl.BlockSpec((1, DVP_C), lambda b: (0, 0)),
      ],
      out_specs=pl.BlockSpec((None, S, WV_C), lambda b: (b, 0, 0)),
      out_shape=jax.ShapeDtypeStruct((B, S, WV_C), BF16),
      scratch_shapes=[
          pltpu.VMEM((S // C, H_C, 2 * DKP_C, DVP_C), F32),
          pltpu.VMEM((H_C, C, C), F32),
          pltpu.VMEM((2, C, WQ_C), F32),
      ],
      compiler_params=_cparams(1),
      name="retention",
  )(qc, kct, vc, gc, rdq, rdh, gn)


def _outproj_kernel(oa_ref, ob_ref, oc_ref, w_ref, x_ref, gate_ref, o_ref):
  y = jnp.dot(oa_ref[...], w_ref[0:W_A, :], preferred_element_type=F32)
  y += jnp.dot(ob_ref[...], w_ref[W_A:W_A + W_B, :], preferred_element_type=F32)
  y += jnp.dot(oc_ref[...], w_ref[W_A + W_B:OUT_ROWS_P, :], preferred_element_type=F32)
  o_ref[...] = x_ref[...] + gate_ref[...] * y


def _outproj(oa, ob, oc, w, x, gate):
  B, S, D = x.shape
  tm = min(OUT_ROW_TILE, S)

  def rows(width):
    return pl.BlockSpec((None, tm, width), lambda b, i: (b, i, 0))

  return pl.pallas_call(
      _outproj_kernel,
      grid=(B, S // tm),
      in_specs=[rows(W_A), rows(W_B), rows(WV_C),
                pl.BlockSpec((OUT_ROWS_P, D), lambda b, i: (0, 0)),
                rows(D),
                pl.BlockSpec((None, 1, D), lambda b, i: (b, 0, 0))],
      out_specs=rows(D),
      out_shape=jax.ShapeDtypeStruct((B, S, D), F32),
      compiler_params=_cparams(2),
      name="outproj",
  )(oa, ob, oc, w, x, gate)


def _rope_tables(S, theta, rot_dim, group):
  half = rot_dim // 2
  pos = jnp.arange(S, dtype=F32)
  inv = theta ** (-jnp.arange(0, rot_dim, 2, dtype=F32) / rot_dim)
  ang = pos[:, None] * inv[None, :]
  cos, sin = jnp.cos(ang), jnp.sin(ang)
  g = np.arange(LANES) % group
  idx = g % half
  in_first = jnp.asarray(g < half)[None, :]
  in_second = jnp.asarray((g >= half) & (g < rot_dim))[None, :]
  c = jnp.where(in_first | in_second, cos[:, idx], 1.0)
  s1 = jnp.where(in_first, -sin[:, idx], 0.0)
  s2 = jnp.where(in_second, sin[:, idx], 0.0)
  return jnp.stack([c, s1, s2]).astype(F32)


def _pad_heads(w, n_heads, d, d_pad, axis):
  pieces = []
  zshape = list(w.shape)
  zshape[axis] = d_pad - d
  zeros = jnp.zeros(zshape, w.dtype)
  for h in range(n_heads):
    pieces += [lax.slice_in_dim(w, h * d, (h + 1) * d, axis=axis), zeros]
  return pieces


def _prep_w_in(w_in):
  w_in = w_in.astype(BF16)
  offs = np.concatenate([[0], np.cumsum(SPLIT_SIZES)])
  ax = w_in.ndim - 1
  pieces = [w_in[..., :offs[8]]]
  for i, (d, d_pad) in zip(range(8, 12), [(DK_C, DKP_C), (DK_C, DKP_C), (DV_C, DVP_C), (DV_C, DVP_C)]):
    pieces += _pad_heads(w_in[..., offs[i]:offs[i + 1]], H_C, d, d_pad, ax)
  return jnp.concatenate(pieces, axis=ax)


def _prep_w_out(w_out):
  w_out = w_out.astype(BF16)
  pieces = [w_out[:, :W_A + W_B]] + _pad_heads(w_out[:, W_A + W_B:], H_C, DV_C, DVP_C, 1)
  return jnp.concatenate(pieces, axis=1)


@jax.jit
def _forward(x, c, norm_g, w_ada, b_ada, w_in, w_out, qn_a, kn_a, lambda_q1, lambda_k1,
             lambda_q2, lambda_k2, subln_a, qn_b, kn_b, ret_decay, gn_c):
  B, S, D = x.shape
  L = w_in.shape[0]
  rows = 8 * ((B + 7) // 8)
  c_pad = jnp.pad(c, ((0, rows - B), (0, 0)))
  mod = _adaln(c_pad, w_ada, b_ada)[:, :B]
  w_in_p = _prep_w_in(w_in)
  w_out_p = _prep_w_out(w_out)
  taba = _rope_tables(S, ROT_THETA, ROT_A, DK_A)
  tabb = _rope_tables(S, ROT_THETA, ROT_B, DH_B)
  tabc = _rope_tables(S, RET_THETA, DK_C, DKP_C)
  gn_p = jnp.pad(gn_c, ((0, 0), (0, DVP_C - DV_C)))
  for l in range(L):
    shift = mod[l, :, None, 0:D]
    scale = mod[l, :, None, D:2 * D]
    gate = mod[l, :, None, 2 * D:3 * D]
    gains = jnp.stack([jnp.tile(qn_a[l], LANES // DK_A), jnp.tile(kn_a[l], LANES // DK_A),
                       jnp.tile(qn_b[l], LANES // DH_B), jnp.tile(kn_b[l], LANES // DH_B)])
    (qa, ka, va, ga, qb, kb, vb, gb, qc, kc, vc, gc) = _inproj(
        x, norm_g[l][None, :], shift, scale, w_in_p[l], gains, taba, tabb, tabc)
    lam_init = 0.8 - 0.6 * math.exp(-0.3 * l)
    lam_params = jnp.stack([lambda_q1[l], lambda_k1[l], lambda_q2[l], lambda_k2[l]])
    oa = _attn_a(qa, ka, va, ga, lam_params, jnp.tile(subln_a[l], H_A)[None, :], lam_init)
    ob = _attn_b(qb, kb, vb, gb)
    rdq = jnp.repeat(ret_decay[l], DKP_C, axis=1)
    rdh = jnp.broadcast_to(ret_decay[l].reshape(2 * H_C, 1), (2 * H_C, RET_CHUNK))
    oc = _retention(qc, kc, vc, gc, rdq, rdh, gn_p[l][None, :])
    x = _outproj(oa, ob, oc, w_out_p[l], x, gate)
  return x


def kernel(x, c, norm_g, w_ada, b_ada, w_in, w_out, qn_a, kn_a, lambda_q1, lambda_k1, lambda_q2,
           lambda_k2, subln_a, qn_b, kn_b, ret_decay, gn_c):
  return _forward(x, c, norm_g, w_ada, b_ada, w_in, w_out, qn_a, kn_a, lambda_q1, lambda_k1,
                  lambda_q2, lambda_k2, subln_a, qn_b, kn_b, ret_decay, gn_c)
```

```python
import functools
import math

import numpy as np
import jax
import jax.numpy as jnp
from jax import lax
from jax.experimental import pallas as pl
from jax.experimental.pallas import tpu as pltpu

F32 = jnp.float32
BF16 = jnp.bfloat16

H_A, DK_A, DV_A = 4, 32, 64
H_B, DH_B = 6, 64
H_C, DK_C, DV_C = 4, 48, 96
SPLIT_SIZES = (256, 256, 256, 256, 384, 384, 384, 384, 192, 192, 384, 384)
ROT_THETA = 500000.0
ROT_A = DK_A // 4
ROT_B = DH_B // 4
RET_THETA = 10000.0
N_SIDE = 64
DILATIONS = (1, 4, 16)
EPS = 1e-6
NEG = -1e30

LANES = 128
MXU_DIM = 256
V7X_VMEM_BYTES = 64 * 1024 * 1024
VMEM_LIMIT = V7X_VMEM_BYTES * 7 // 8

DKP_C = 64
DVP_C = LANES
W_A = H_A * DV_A
W_B = H_B * DH_B
WQ_C = H_C * DKP_C
WV_C = H_C * DVP_C
C_QA, C_KA, C_VA, C_GA = 0, 256, 512, 768
C_QB, C_KB, C_VB, C_GB = 1024, 1408, 1792, 2176
C_QC, C_KC, C_VC, C_GC = 2560, 2816, 3072, 3584
IN_COLS_P = 4096
OUT_ROWS_P = W_A + W_B + WV_C

RET_CHUNK = 256
Q_TILE_A = 256
K_CHUNK_A = 256
LOOKAHEAD_A = 9
VX_ROWS_A = DV_A + 16
LOG2_E = math.log2(math.e)
Q_BLOCK_B = 128
K_WIN_B = Q_BLOCK_B + 2 * N_SIDE
WINDOW_OFFSETS_B = (-N_SIDE, 0, -2 * N_SIDE)
ITEMS_B = 4
ROW_TILE = 512
OUT_ROW_TILE = 2048

_NT = (((1,), (1,)), ((), ()))


def _cparams(n_grid):
  return pltpu.CompilerParams(
      dimension_semantics=("arbitrary",) * n_grid, vmem_limit_bytes=VMEM_LIMIT)


def _group_ones(group, width=LANES):
  shift = int(math.log2(group))
  r = lax.broadcasted_iota(jnp.int32, (width, width), 0) >> shift
  c = lax.broadcasted_iota(jnp.int32, (width, width), 1) >> shift
  return (r == c).astype(BF16)


def _group_sumsq(t, group):
  t2 = (t * t).astype(BF16)
  parts = []
  for c0 in range(0, t.shape[1], MXU_DIM):
    w = min(MXU_DIM, t.shape[1] - c0)
    parts.append(jnp.dot(t2[:, c0:c0 + w], _group_ones(group, w), preferred_element_type=F32))
  return parts[0] if len(parts) == 1 else jnp.concatenate(parts, axis=1)


def _rope_tile(t, tab_ref, half):
  up = pltpu.roll(t, LANES - half, 1)
  down = pltpu.roll(t, half, 1)
  return t * tab_ref[0] + up * tab_ref[1] + down * tab_ref[2]


def _silu(t):
  return t * jax.nn.sigmoid(t)


def _adaln_kernel(c_ref, w_ref, b_ref, o_ref):
  cs = _silu(c_ref[...])
  o_ref[0] = jnp.dot(cs, w_ref[0], preferred_element_type=F32,
                     precision=lax.Precision.HIGHEST) + b_ref[0]


def _adaln(c_pad, w_ada, b_ada):
  L, D, _ = w_ada.shape
  rows = c_pad.shape[0]
  return pl.pallas_call(
      _adaln_kernel,
      grid=(L, 3),
      in_specs=[
          pl.BlockSpec((rows, D), lambda l, j: (0, 0)),
          pl.BlockSpec((1, D, D), lambda l, j: (l, 0, j)),
          pl.BlockSpec((1, 1, D), lambda l, j: (l, 0, j)),
      ],
      out_specs=pl.BlockSpec((1, rows, D), lambda l, j: (l, 0, j)),
      out_shape=jax.ShapeDtypeStruct((L, rows, 3 * D), F32),
      compiler_params=_cparams(2),
      name="adaln",
  )(c_pad, w_ada, b_ada.reshape(L, 1, 3 * D))


def _inproj_kernel(x_ref, ng_ref, sh_ref, sc_ref, wab_ref, wc_ref, gains_ref, taba_ref, tabb_ref, tabc_ref,
                   qa_ref, ka_ref, va_ref, ga_ref, qb_ref, kb_ref, vb_ref, gb_ref,
                   qc_ref, kc_ref, vc_ref, gc_ref, hb_ref):
  x = x_ref[...]
  ms = jnp.mean(x * x, axis=-1, keepdims=True)
  h = x * lax.rsqrt(ms + EPS) * ng_ref[...]
  h = h * (1.0 + sc_ref[...]) + sh_ref[...]
  hb_ref[...] = h.astype(BF16)

  def proj(c0, width):
    w = wab_ref[:, c0:c0 + width] if c0 < C_QC else wc_ref[:, c0 - C_QC:c0 - C_QC + width]
    return jnp.dot(hb_ref[...], w, preferred_element_type=F32)

  def tiles(p):
    return [p[:, LANES * t:LANES * (t + 1)] for t in range(p.shape[1] // LANES)]

  gains = gains_ref[...]

  def normed_tiles(p, group, gain_row):
    y = p * lax.rsqrt(_group_sumsq(p, group) * (1.0 / group) + EPS)
    return [t * gain_row for t in tiles(y)]

  for t, p in enumerate(normed_tiles(proj(C_QA, W_A), DK_A, gains[0:1])):
    y = _rope_tile(p, taba_ref, ROT_A // 2) * (DK_A ** -0.5 * LOG2_E)
    qa_ref[LANES * t:LANES * (t + 1), :] = y.T.astype(BF16)
  for t, p in enumerate(normed_tiles(proj(C_KA, W_A), DK_A, gains[1:2])):
    y = _rope_tile(p, taba_ref, ROT_A // 2)
    ka_ref[:, LANES * t:LANES * (t + 1)] = y.astype(BF16)
  for t, p in enumerate(tiles(proj(C_VA, W_A))):
    va_ref[LANES * t:LANES * (t + 1), :] = p.T.astype(BF16)
  ga_ref[...] = _silu(proj(C_GA, W_A)).astype(BF16)

  for t, p in enumerate(normed_tiles(proj(C_QB, W_B), DH_B, gains[2:3])):
    qb_ref[t] = _rope_tile(p, tabb_ref, ROT_B // 2) * (DH_B ** -0.5 * LOG2_E)
  for t, p in enumerate(normed_tiles(proj(C_KB, W_B), DH_B, gains[3:4])):
    kb_ref[t] = _rope_tile(p, tabb_ref, ROT_B // 2)
  for t, p in enumerate(tiles(proj(C_VB, W_B))):
    vb_ref[t] = p
  gb_ref[...] = _silu(proj(C_GB, W_B)).astype(BF16)

  for t, p in enumerate(tiles(proj(C_QC, WQ_C))):
    qc_ref[:, LANES * t:LANES * (t + 1)] = _rope_tile(p, tabc_ref, DK_C // 2).astype(BF16)
  for t, p in enumerate(tiles(proj(C_KC, WQ_C))):
    y = _rope_tile(p, tabc_ref, DK_C // 2) * (DK_C ** -0.5)
    kc_ref[LANES * t:LANES * (t + 1), :] = y.T.astype(BF16)
  vc_ref[...] = proj(C_VC, WV_C).astype(BF16)
  gc_ref[...] = _silu(proj(C_GC, WV_C)).astype(BF16)


def _inproj(x, ng, shift, scale, w_ab, w_c, gains, taba, tabb, tabc):
  B, S, D = x.shape
  tm = min(ROW_TILE, S)
  n_pair = W_B // LANES

  def rows(width, dtype=BF16):
    return (pl.BlockSpec((None, tm, width), lambda i, b: (b, i, 0)),
            jax.ShapeDtypeStruct((B, S, width), dtype))

  def slabs():
    return (pl.BlockSpec((None, n_pair, tm, LANES), lambda i, b: (b, 0, i, 0)),
            jax.ShapeDtypeStruct((B, n_pair, S, LANES), F32))

  def feature_major(width):
    return (pl.BlockSpec((None, width, tm), lambda i, b: (b, 0, i)),
            jax.ShapeDtypeStruct((B, width, S), BF16))

  outs = [feature_major(W_A), rows(W_A), feature_major(W_A), rows(W_A),
          slabs(), slabs(), slabs(), rows(W_B),
          rows(WQ_C), feature_major(WQ_C), rows(WV_C), rows(WV_C)]
  tab_spec = pl.BlockSpec((3, tm, LANES), lambda i, b: (0, i, 0))
  return pl.pallas_call(
      _inproj_kernel,
      grid=(S // tm, B),
      in_specs=[
          pl.BlockSpec((None, tm, D), lambda i, b: (b, i, 0)),
          pl.BlockSpec((1, D), lambda i, b: (0, 0)),
          pl.BlockSpec((None, 1, D), lambda i, b: (b, 0, 0)),
          pl.BlockSpec((None, 1, D), lambda i, b: (b, 0, 0)),
          pl.BlockSpec((D, C_QC), lambda i, b: (0, 0)),
          pl.BlockSpec((D, IN_COLS_P - C_QC), lambda i, b: (0, 0)),
          pl.BlockSpec((4, LANES), lambda i, b: (0, 0)),
          tab_spec, tab_spec, tab_spec,
      ],
      out_specs=[o[0] for o in outs],
      out_shape=[o[1] for o in outs],
      scratch_shapes=[pltpu.VMEM((tm, D), BF16)],
      compiler_params=_cparams(2),
      name="inproj",
  )(x, ng, shift, scale, w_ab, w_c, gains, taba, tabb, tabc)


def _attn_a_kernel(qt_ref, k_ref, vt_ref, g_ref, lam_ref, sub_ref, o_ref,
                   vx_ref, ot_ref, *, lam_init):
  S = k_ref.shape[0]
  nq = qt_ref.shape[1]
  tk = min(K_CHUNK_A, S)

  @pl.when(pl.program_id(1) == 0)
  def _():
    for h in range(H_A):
      vx_ref[h, 0:DV_A, :] = vt_ref[DV_A * h:DV_A * (h + 1), :]
      vx_ref[h, DV_A:VX_ROWS_A, :] = jnp.ones((VX_ROWS_A - DV_A, S), BF16)

  lp = lam_ref[...]
  lam = (jnp.exp(jnp.sum(lp[0:1] * lp[1:2], axis=-1, keepdims=True))
         - jnp.exp(jnp.sum(lp[2:3] * lp[3:4], axis=-1, keepdims=True)) + lam_init)
  row = lax.broadcasted_iota(jnp.int32, (W_A, 1), 0)

  qt = qt_ref[...]

  def masked_q(h, m):
    lo = h * DV_A + m * DK_A
    return jnp.where((row >= lo) & (row < lo + DK_A), qt, jnp.zeros_like(qt))

  def scores(unit):
    h, m, j = unit
    return jnp.dot(k_ref[j * tk:(j + 1) * tk, :], masked_q(h, m), preferred_element_type=F32)

  units = [(h, m, j) for h in range(H_A) for j in range(S // tk) for m in range(2)]
  pending = [scores(u) for u in units[:LOOKAHEAD_A]]
  run = {}
  for idx, (h, m, j) in enumerate(units):
    st = pending.pop(0)
    if idx + LOOKAHEAD_A < len(units):
      pending.append(scores(units[idx + LOOKAHEAD_A]))
    cm = jnp.max(st, axis=0, keepdims=True)
    vx = vx_ref[h, :, j * tk:(j + 1) * tk]
    m_new = cm if (h, m) not in run else jnp.maximum(run[h, m][0], cm)
    e = jnp.exp2(st - m_new).astype(BF16)
    pv = jnp.dot(vx, e, preferred_element_type=F32)
    if (h, m) not in run:
      run[h, m] = (m_new, pv)
    else:
      m_old, acc = run[h, m]
      run[h, m] = (m_new, jnp.exp2(m_old - m_new) * acc + pv)
    if m == 1 and j == S // tk - 1:
      maps = [run[h, mm][1] for mm in range(2)]
      maps = [acc[0:DV_A] / acc[DV_A:DV_A + 1] for acc in maps]
      ot_ref[DV_A * h:DV_A * (h + 1), :] = maps[0] - lam * maps[1]

  for t in range(W_A // LANES):
    cols = slice(LANES * t, LANES * (t + 1))
    o = ot_ref[cols, :].T
    y = o * lax.rsqrt(_group_sumsq(o, DV_A) * (1.0 / DV_A) + EPS) * sub_ref[:, cols]
    y = y * (1.0 - lam_init) * g_ref[:, cols].astype(F32)
    o_ref[:, cols] = y.astype(BF16)


def _attn_a(qat, ka, vat, ga, lam_params, sub, lam_init):
  B, S, _ = ka.shape
  tq = min(Q_TILE_A, S)
  return pl.pallas_call(
      functools.partial(_attn_a_kernel, lam_init=lam_init),
      grid=(B, S // tq),
      in_specs=[
          pl.BlockSpec((None, W_A, tq), lambda b, i: (b, 0, i)),
          pl.BlockSpec((None, S, W_A), lambda b, i: (b, 0, 0)),
          pl.BlockSpec((None, W_A, S), lambda b, i: (b, 0, 0)),
          pl.BlockSpec((None, tq, W_A), lambda b, i: (b, i, 0)),
          pl.BlockSpec((4, DK_A), lambda b, i: (0, 0)),
          pl.BlockSpec((1, W_A), lambda b, i: (0, 0)),
      ],
      out_specs=pl.BlockSpec((None, tq, W_A), lambda b, i: (b, i, 0)),
      out_shape=jax.ShapeDtypeStruct((B, S, W_A), BF16),
      scratch_shapes=[
          pltpu.VMEM((H_A, VX_ROWS_A, S), BF16),
          pltpu.VMEM((W_A, tq), F32),
      ],
      compiler_params=_cparams(2),
      name="attn_a",
  )(qat, ka, vat, ga, lam_params, sub)


def _attn_b_kernel(q_ref, k_ref, v_ref, g_ref, o_ref, m_ref, l_ref, acc_ref, bias_ref):
  S = q_ref.shape[0]
  lane = lax.broadcasted_iota(jnp.int32, (1, LANES), 1)
  first = lane < DH_B
  ones = jnp.ones((K_WIN_B, LANES), BF16)

  def band(n_q, off):
    qi = lax.broadcasted_iota(jnp.int32, (2 * n_q, K_WIN_B), 0) & (n_q - 1)
    kj = lax.broadcasted_iota(jnp.int32, (2 * n_q, K_WIN_B), 1)
    return jnp.where(jnp.abs(kj - qi + off) <= N_SIDE, 0.0, NEG).astype(F32)

  for idx, off in enumerate(WINDOW_OFFSETS_B):
    bias_ref[idx, 0:2 * Q_BLOCK_B, :] = band(Q_BLOCK_B, off)
  bias_ref[len(WINDOW_OFFSETS_B)] = band(K_WIN_B, 0)

  for gi, d in enumerate(sorted(DILATIONS, reverse=True)):
    class_len = S // d
    qblk = class_len if class_len <= K_WIN_B else Q_BLOCK_B
    n_qblk = class_len // qblk
    n_items = min(ITEMS_B * Q_BLOCK_B // qblk, d * n_qblk)
    top, bot = slice(0, qblk), slice(qblk, 2 * qblk)
    shape = (qblk, LANES)

    def items(ib, carry, d=d, gi=gi, class_len=class_len, qblk=qblk, n_qblk=n_qblk, n_items=n_items,
              top=top, bot=bot, shape=shape):
      loaded = []
      for u in range(n_items):
        i = ib * n_items + u
        r = i // n_qblk
        a0 = (i % n_qblk) * qblk
        ak = jnp.clip(a0 - N_SIDE, 0, class_len - K_WIN_B)
        if d == 1:
          q_rows = pl.ds(pl.multiple_of(a0, qblk), qblk)
          k_rows = pl.ds(pl.multiple_of(ak, N_SIDE), K_WIN_B)
        else:
          q_rows = pl.ds(d * a0 + r, qblk, stride=d)
          k_rows = pl.ds(d * ak + r, K_WIN_B, stride=d)
        qt = q_ref[q_rows, :]
        q2 = jnp.concatenate([jnp.where(first, qt, 0.0), jnp.where(first, 0.0, qt)], axis=0).astype(BF16)
        kt = k_ref[k_rows, :].astype(BF16)
        vt = jnp.concatenate([v_ref[k_rows, :].astype(BF16), ones], axis=1)
        if qblk == K_WIN_B:
          which = len(WINDOW_OFFSETS_B)
        else:
          which = jnp.where(a0 == 0, 1, jnp.where(a0 == class_len - qblk, 2, 0))
        loaded.append((q_rows, which, q2, kt, vt))
      scores = [lax.dot_general(q2, kt, _NT, preferred_element_type=F32) for _, _, q2, kt, _ in loaded]
      probs = []
      for (_, which, _, _, _), s in zip(loaded, scores):
        s = s + bias_ref[which, 0:2 * qblk, :]
        m = jnp.max(s, axis=-1, keepdims=True)
        probs.append((m, jnp.exp2(s - m).astype(BF16)))
      outs = [jnp.dot(e, vt, preferred_element_type=F32) for (_, e), (_, _, _, _, vt) in zip(probs, loaded)]
      for (q_rows, _, _, _, _), (m, _), pv in zip(loaded, probs, outs):
        m_new = jnp.where(first, jnp.broadcast_to(m[top], shape), jnp.broadcast_to(m[bot], shape))
        l_new = jnp.where(first, pv[top, LANES:], pv[bot, LANES:])
        pv_new = jnp.where(first, pv[top, :LANES], pv[bot, :LANES])
        if gi == 0:
          m_ref[q_rows, :] = m_new
          l_ref[q_rows, :] = l_new
          acc_ref[q_rows, :] = pv_new
        else:
          m_old = m_ref[q_rows, :]
          m_tot = jnp.maximum(m_old, m_new)
          w_old = jnp.exp2(m_old - m_tot)
          w_new = jnp.exp2(m_new - m_tot)
          m_ref[q_rows, :] = m_tot
          l_ref[q_rows, :] = w_old * l_ref[q_rows, :] + w_new * l_new
          acc_ref[q_rows, :] = w_old * acc_ref[q_rows, :] + w_new * pv_new
      return carry

    lax.fori_loop(0, d * n_qblk // n_items, items, 0)

  def finish(c, carry):
    rows = pl.ds(pl.multiple_of(c * ROW_TILE, ROW_TILE), ROW_TILE)
    y = acc_ref[rows, :] / l_ref[rows, :] * g_ref[rows, :].astype(F32)
    o_ref[rows, :] = y.astype(BF16)
    return carry

  lax.fori_loop(0, S // ROW_TILE, finish, 0)


def _attn_b(qb, kb, vb, gb):
  B, n_pair, S, _ = qb.shape
  slab = pl.BlockSpec((None, None, S, LANES), lambda b, j: (b, j, 0, 0))
  cols = pl.BlockSpec((None, S, LANES), lambda b, j: (b, 0, j))
  return pl.pallas_call(
      _attn_b_kernel,
      grid=(B, n_pair),
      in_specs=[slab, slab, slab, cols],
      out_specs=cols,
      out_shape=jax.ShapeDtypeStruct((B, S, W_B), BF16),
      scratch_shapes=[pltpu.VMEM((S, LANES), F32)] * 3
      + [pltpu.VMEM((len(WINDOW_OFFSETS_B) + 1, 2 * K_WIN_B, K_WIN_B), F32)],
      compiler_params=_cparams(2),
      name="attn_b",
  )(qb, kb, vb, gb)


def _log_sigmoid(t):
  return jnp.minimum(t, 0.0) - jnp.log1p(jnp.exp(-jnp.abs(t)))


def _ret_kernel(q_ref, kt_ref, v_ref, g_ref, rdq_ref, rdh_ref, gn_ref, o_ref, u_ref, d_ref, tab_ref):
  S = q_ref.shape[0]
  C = RET_CHUNK
  N = S // C
  lgq = _log_sigmoid(rdq_ref[...])
  lgh = _log_sigmoid(rdh_ref[...])
  pos = lax.broadcasted_iota(jnp.int32, (C, 1), 0).astype(F32)
  tab_ref[0] = jnp.exp(lgq[0:1] * (pos + 1.0))
  tab_ref[1] = jnp.exp(lgq[1:2] * (C - pos))
  pos_row = lax.broadcasted_iota(jnp.int32, (1, C), 1).astype(F32)
  kdec = jnp.concatenate([jnp.exp(lgh[0:H_C] * (C - 1.0 - pos_row)), jnp.exp(lgh[H_C:] * pos_row)], axis=0)
  dd = (lax.broadcasted_iota(jnp.int32, (C, C), 0) - lax.broadcasted_iota(jnp.int32, (C, C), 1))
  dist = jnp.abs(dd).astype(F32)
  for h in range(H_C):
    d_ref[h] = jnp.exp(jnp.where(dd >= 0, lgh[h:h + 1], lgh[H_C + h:H_C + h + 1]) * dist)
  cdec = jnp.exp(lgh[:, 0:LANES] * float(C))

  lane = lax.broadcasted_iota(jnp.int32, (1, LANES), 1)
  first = lane < DKP_C

  def stack(fwd, bwd, hh):
    if hh == 0:
      return jnp.where(first, fwd, pltpu.roll(bwd, DKP_C, 1))
    return jnp.where(first, pltpu.roll(fwd, DKP_C, 1), bwd)

  def chunk_rows(n):
    return pl.ds(pl.multiple_of(n * C, C), C)

  def local_state(n, carry):
    rows = chunk_rows(n)
    stacks = []
    for h in range(H_C):
      kt = kt_ref[DKP_C * h:DKP_C * (h + 1), rows].astype(F32)
      stacks.append(jnp.concatenate([kt * kdec[h:h + 1], kt * kdec[H_C + h:H_C + h + 1]],
                                    axis=0).astype(BF16))
    states = [jnp.dot(st, v_ref[rows, DVP_C * h:DVP_C * (h + 1)], preferred_element_type=F32)
              for h, st in enumerate(stacks)]
    for h, u in enumerate(states):
      u_ref[n, h] = u
    return carry

  lax.fori_loop(0, N, local_state, 0)

  zero = tuple(jnp.zeros((DKP_C, DVP_C), F32) for _ in range(H_C))

  def scan_fwd(n, state):
    new = []
    for h in range(H_C):
      u = u_ref[n, h, 0:DKP_C, :]
      u_ref[n, h, 0:DKP_C, :] = state[h]
      new.append(state[h] * cdec[h:h + 1] + u)
    return tuple(new)

  lax.fori_loop(0, N, scan_fwd, zero)

  def scan_bwd(i, state):
    n = N - 1 - i
    new = []
    for h in range(H_C):
      u = u_ref[n, h, DKP_C:2 * DKP_C, :]
      u_ref[n, h, DKP_C:2 * DKP_C, :] = state[h]
      new.append(state[h] * cdec[H_C + h:H_C + h + 1] + u)
    return tuple(new)

  lax.fori_loop(0, N, scan_bwd, zero)

  def outputs(n, carry):
    rows = chunk_rows(n)
    qms, kts, qss = [], [], []
    for t in range(WQ_C // LANES):
      cols = slice(LANES * t, LANES * (t + 1))
      qt = q_ref[rows, cols]
      q32 = qt.astype(F32)
      qf = q32 * tab_ref[0, :, cols]
      qb = q32 * tab_ref[1, :, cols]
      for hh in range(2):
        keep = first if hh == 0 else jnp.logical_not(first)
        qms.append(jnp.where(keep, qt, jnp.zeros_like(qt)))
        kts.append(kt_ref[cols, rows])
        qss.append(stack(qf, qb, hh).astype(BF16))
    vcols = [slice(DVP_C * h, DVP_C * (h + 1)) for h in range(H_C)]
    atts = [jnp.dot(qms[h], kts[h], preferred_element_type=F32) for h in range(H_C)]
    cross = [jnp.dot(qss[h], u_ref[n, h].astype(BF16), preferred_element_type=F32) for h in range(H_C)]
    atts = [(atts[h] * d_ref[h]).astype(BF16) for h in range(H_C)]
    inner = [jnp.dot(atts[h], v_ref[rows, vcols[h]], preferred_element_type=F32) for h in range(H_C)]
    for h in range(H_C):
      o = inner[h] + cross[h]
      ms = jnp.sum(o * o, axis=-1, keepdims=True) * (1.0 / DV_C)
      y = o * lax.rsqrt(ms + EPS) * gn_ref[...]
      o_ref[rows, vcols[h]] = (y * g_ref[rows, vcols[h]].astype(F32)).astype(BF16)
    return carry

  lax.fori_loop(0, N, outputs, 0)


def _retention(qc, kct, vc, gc, rdq, rdh, gn):
  B, S, _ = qc.shape
  C = RET_CHUNK
  return pl.pallas_call(
      _ret_kernel,
      grid=(B,),
      in_specs=[
          pl.BlockSpec((None, S, WQ_C), lambda b: (b, 0, 0)),
          pl.BlockSpec((None, WQ_C, S), lambda b: (b, 0, 0)),
          pl.BlockSpec((None, S, WV_C), lambda b: (b, 0, 0)),
          pl.BlockSpec((None, S, WV_C), lambda b: (b, 0, 0)),
          pl.BlockSpec((2, WQ_C), lambda b: (0, 0)),
          pl.BlockSpec((2 * H_C, C), lambda b: (0, 0)),
          pl.BlockSpec((1, DVP_C), lambda b: (0, 0)),
      ],
      out_specs=pl.BlockSpec((None, S, WV_C), lambda b: (b, 0, 0)),
      out_shape=jax.ShapeDtypeStruct((B, S, WV_C), BF16),
      scratch_shapes=[
          pltpu.VMEM((S // C, H_C, 2 * DKP_C, DVP_C), F32),
          pltpu.VMEM((H_C, C, C), F32),
          pltpu.VMEM((2, C, WQ_C), F32),
      ],
      compiler_params=_cparams(1),
      name="retention",
  )(qc, kct, vc, gc, rdq, rdh, gn)


def _outproj_kernel(oa_ref, ob_ref, oc_ref, w_ref, x_ref, gate_ref, o_ref):
  y = jnp.dot(oa_ref[...], w_ref[0:W_A, :], preferred_element_type=F32)
  y += jnp.dot(ob_ref[...], w_ref[W_A:W_A + W_B, :], preferred_element_type=F32)
  y += jnp.dot(oc_ref[...], w_ref[W_A + W_B:OUT_ROWS_P, :], preferred_element_type=F32)
  o_ref[...] = x_ref[...] + gate_ref[...] * y


def _outproj(oa, ob, oc, w, x, gate):
  B, S, D = x.shape
  tm = min(OUT_ROW_TILE, S)

  def rows(width):
    return pl.BlockSpec((None, tm, width), lambda b, i: (b, i, 0))

  return pl.pallas_call(
      _outproj_kernel,
      grid=(B, S // tm),
      in_specs=[rows(W_A), rows(W_B), rows(WV_C),
                pl.BlockSpec((OUT_ROWS_P, D), lambda b, i: (0, 0)),
                rows(D),
                pl.BlockSpec((None, 1, D), lambda b, i: (b, 0, 0))],
      out_specs=rows(D),
      out_shape=jax.ShapeDtypeStruct((B, S, D), F32),
      compiler_params=_cparams(2),
      name="outproj",
  )(oa, ob, oc, w, x, gate)


def _rope_tables(S, theta, rot_dim, group):
  half = rot_dim // 2
  pos = jnp.arange(S, dtype=F32)
  inv = theta ** (-jnp.arange(0, rot_dim, 2, dtype=F32) / rot_dim)
  ang = pos[:, None] * inv[None, :]
  cos, sin = jnp.cos(ang), jnp.sin(ang)
  g = np.arange(LANES) % group
  idx = g % half
  in_first = jnp.asarray(g < half)[None, :]
  in_second = jnp.asarray((g >= half) & (g < rot_dim))[None, :]
  c = jnp.where(in_first | in_second, cos[:, idx], 1.0)
  s1 = jnp.where(in_first, -sin[:, idx], 0.0)
  s2 = jnp.where(in_second, sin[:, idx], 0.0)
  return jnp.stack([c, s1, s2]).astype(F32)


def _pad_heads(w, n_heads, d, d_pad, axis):
  shape = w.shape[:axis] + (n_heads, d) + w.shape[axis + 1:]
  pad = [(0, 0)] * (len(shape))
  pad[axis + 1] = (0, d_pad - d)
  out = jnp.pad(w.reshape(shape), pad)
  return out.reshape(w.shape[:axis] + (n_heads * d_pad,) + w.shape[axis + 1:])


def _prep_w_in(w_in):
  offs = np.concatenate([[0], np.cumsum(SPLIT_SIZES)])
  ax = w_in.ndim - 1
  w_ab = w_in[..., :offs[8]].astype(BF16)
  w_c = w_in[..., offs[8]:].astype(BF16)
  rel = offs[8:] - offs[8]
  sec = [w_c[..., rel[i]:rel[i + 1]] for i in range(4)]
  sec = [_pad_heads(sec[0], H_C, DK_C, DKP_C, ax), _pad_heads(sec[1], H_C, DK_C, DKP_C, ax),
         _pad_heads(sec[2], H_C, DV_C, DVP_C, ax), _pad_heads(sec[3], H_C, DV_C, DVP_C, ax)]
  return w_ab, jnp.concatenate(sec, axis=ax)


def _prep_w_out(w_out):
  w_out = w_out.astype(BF16)
  top = w_out[:, :W_A + W_B]
  bot = _pad_heads(w_out[:, W_A + W_B:], H_C, DV_C, DVP_C, 1)
  return jnp.concatenate([top, bot], axis=1)


@jax.jit
def _forward(x, c, norm_g, w_ada, b_ada, w_in, w_out, qn_a, kn_a, lambda_q1, lambda_k1,
             lambda_q2, lambda_k2, subln_a, qn_b, kn_b, ret_decay, gn_c):
  B, S, D = x.shape
  L = w_in.shape[0]
  rows = 8 * ((B + 7) // 8)
  c_pad = jnp.pad(c, ((0, rows - B), (0, 0)))
  mod = _adaln(c_pad, w_ada, b_ada)[:, :B]
  w_ab, w_c = _prep_w_in(w_in)
  w_out_p = _prep_w_out(w_out)
  taba = _rope_tables(S, ROT_THETA, ROT_A, DK_A)
  tabb = _rope_tables(S, ROT_THETA, ROT_B, DH_B)
  tabc = _rope_tables(S, RET_THETA, DK_C, DKP_C)
  gn_p = jnp.pad(gn_c, ((0, 0), (0, DVP_C - DV_C)))
  for l in range(L):
    shift = mod[l, :, None, 0:D]
    scale = mod[l, :, None, D:2 * D]
    gate = mod[l, :, None, 2 * D:3 * D]
    gains = jnp.stack([jnp.tile(qn_a[l], LANES // DK_A), jnp.tile(kn_a[l], LANES // DK_A),
                       jnp.tile(qn_b[l], LANES // DH_B), jnp.tile(kn_b[l], LANES // DH_B)])
    (qa, ka, va, ga, qb, kb, vb, gb, qc, kc, vc, gc) = _inproj(
        x, norm_g[l][None, :], shift, scale, w_ab[l], w_c[l], gains, taba, tabb, tabc)
    lam_init = 0.8 - 0.6 * math.exp(-0.3 * l)
    lam_params = jnp.stack([lambda_q1[l], lambda_k1[l], lambda_q2[l], lambda_k2[l]])
    oa = _attn_a(qa, ka, va, ga, lam_params, jnp.tile(subln_a[l], H_A)[None, :], lam_init)
    ob = _attn_b(qb, kb, vb, gb)
    rdq = jnp.repeat(ret_decay[l], DKP_C, axis=1)
    rdh = jnp.broadcast_to(ret_decay[l].reshape(2 * H_C, 1), (2 * H_C, RET_CHUNK))
    oc = _retention(qc, kc, vc, gc, rdq, rdh, gn_p[l][None, :])
    x = _outproj(oa, ob, oc, w_out_p[l], x, gate)
  return x


def kernel(x, c, norm_g, w_ada, b_ada, w_in, w_out, qn_a, kn_a, lambda_q1, lambda_k1, lambda_q2,
           lambda_k2, subln_a, qn_b, kn_b, ret_decay, gn_c):
  return _forward(x, c, norm_g, w_ada, b_ada, w_in, w_out, qn_a, kn_a, lambda_q1, lambda_k1,
                  lambda_q2, lambda_k2, subln_a, qn_b, kn_b, ret_decay, gn_c)
```

```python
import functools
import math

import numpy as np
import jax
import jax.numpy as jnp
from jax import lax
from jax.experimental import pallas as pl
from jax.experimental.pallas import tpu as pltpu

F32 = jnp.float32
BF16 = jnp.bfloat16

H_A, DK_A, DV_A = 4, 32, 64
H_B, DH_B = 6, 64
H_C, DK_C, DV_C = 4, 48, 96
SPLIT_SIZES = (256, 256, 256, 256, 384, 384, 384, 384, 192, 192, 384, 384)
ROT_THETA = 500000.0
ROT_A = DK_A // 4
ROT_B = DH_B // 4
RET_THETA = 10000.0
N_SIDE = 64
DILATIONS = (1, 4, 16)
EPS = 1e-6
NEG = -1e30

LANES = 128
MXU_DIM = 256
V7X_VMEM_BYTES = 64 * 1024 * 1024
VMEM_LIMIT = V7X_VMEM_BYTES * 7 // 8

DKP_C = 64
DVP_C = LANES
W_A = H_A * DV_A
W_B = H_B * DH_B
WQ_C = H_C * DKP_C
WV_C = H_C * DVP_C
C_QA, C_KA, C_VA, C_GA = 0, 256, 512, 768
C_QB, C_KB, C_VB, C_GB = 1024, 1408, 1792, 2176
C_QC, C_KC, C_VC, C_GC = 2560, 2816, 3072, 3584
IN_COLS_P = 4096
OUT_ROWS_P = W_A + W_B + WV_C

RET_CHUNK = 256
Q_TILE_A = 256
K_CHUNK_A = 256
LOOKAHEAD_A = 9
VX_ROWS_A = DV_A + 16
LOG2_E = math.log2(math.e)
Q_BLOCK_B = 128
K_WIN_B = Q_BLOCK_B + 2 * N_SIDE
WINDOW_OFFSETS_B = (-N_SIDE, 0, -2 * N_SIDE)
ITEMS_B = 4
ROW_TILE = 512
OUT_ROW_TILE = 2048

_NT = (((1,), (1,)), ((), ()))


def _cparams(n_grid):
  return pltpu.CompilerParams(
      dimension_semantics=("arbitrary",) * n_grid, vmem_limit_bytes=VMEM_LIMIT)


def _group_ones(group, width=LANES):
  shift = int(math.log2(group))
  r = lax.broadcasted_iota(jnp.int32, (width, width), 0) >> shift
  c = lax.broadcasted_iota(jnp.int32, (width, width), 1) >> shift
  return (r == c).astype(BF16)


def _group_sumsq(t, group):
  t2 = (t * t).astype(BF16)
  parts = []
  for c0 in range(0, t.shape[1], MXU_DIM):
    w = min(MXU_DIM, t.shape[1] - c0)
    parts.append(jnp.dot(t2[:, c0:c0 + w], _group_ones(group, w), preferred_element_type=F32))
  return parts[0] if len(parts) == 1 else jnp.concatenate(parts, axis=1)


def _rope_tile(t, tab_ref, half):
  up = pltpu.roll(t, LANES - half, 1)
  down = pltpu.roll(t, half, 1)
  return t * tab_ref[0] + up * tab_ref[1] + down * tab_ref[2]


def _silu(t):
  return t * jax.nn.sigmoid(t)


def _adaln_kernel(c_ref, w_ref, b_ref, o_ref):
  cs = _silu(c_ref[...])
  o_ref[0] = jnp.dot(cs, w_ref[0], preferred_element_type=F32,
                     precision=lax.Precision.HIGHEST) + b_ref[0]


def _adaln(c_pad, w_ada, b_ada):
  L, D, _ = w_ada.shape
  rows = c_pad.shape[0]
  return pl.pallas_call(
      _adaln_kernel,
      grid=(L, 3),
      in_specs=[
          pl.BlockSpec((rows, D), lambda l, j: (0, 0)),
          pl.BlockSpec((1, D, D), lambda l, j: (l, 0, j)),
          pl.BlockSpec((1, 1, D), lambda l, j: (l, 0, j)),
      ],
      out_specs=pl.BlockSpec((1, rows, D), lambda l, j: (l, 0, j)),
      out_shape=jax.ShapeDtypeStruct((L, rows, 3 * D), F32),
      compiler_params=_cparams(2),
      name="adaln",
  )(c_pad, w_ada, b_ada.reshape(L, 1, 3 * D))


def _inproj_kernel(x_ref, ng_ref, sh_ref, sc_ref, wab_ref, wc_ref, gains_ref, taba_ref, tabb_ref, tabc_ref,
                   qa_ref, ka_ref, va_ref, ga_ref, qb_ref, kb_ref, vb_ref, gb_ref,
                   qc_ref, kc_ref, vc_ref, gc_ref, hb_ref):
  x = x_ref[...]
  ms = jnp.mean(x * x, axis=-1, keepdims=True)
  h = x * lax.rsqrt(ms + EPS) * ng_ref[...]
  h = h * (1.0 + sc_ref[...]) + sh_ref[...]
  hb_ref[...] = h.astype(BF16)

  def proj(c0, width):
    w = wab_ref[:, c0:c0 + width] if c0 < C_QC else wc_ref[:, c0 - C_QC:c0 - C_QC + width]
    return jnp.dot(hb_ref[...], w, preferred_element_type=F32)

  def tiles(p):
    return [p[:, LANES * t:LANES * (t + 1)] for t in range(p.shape[1] // LANES)]

  gains = gains_ref[...]

  def normed_tiles(p, group, gain_row):
    y = p * lax.rsqrt(_group_sumsq(p, group) * (1.0 / group) + EPS)
    return [t * gain_row for t in tiles(y)]

  for t, p in enumerate(normed_tiles(proj(C_QA, W_A), DK_A, gains[0:1])):
    y = _rope_tile(p, taba_ref, ROT_A // 2) * (DK_A ** -0.5 * LOG2_E)
    qa_ref[LANES * t:LANES * (t + 1), :] = y.T.astype(BF16)
  for t, p in enumerate(normed_tiles(proj(C_KA, W_A), DK_A, gains[1:2])):
    y = _rope_tile(p, taba_ref, ROT_A // 2)
    ka_ref[:, LANES * t:LANES * (t + 1)] = y.astype(BF16)
  for t, p in enumerate(tiles(proj(C_VA, W_A))):
    va_ref[LANES * t:LANES * (t + 1), :] = p.T.astype(BF16)
  ga_ref[...] = _silu(proj(C_GA, W_A)).astype(BF16)

  for t, p in enumerate(normed_tiles(proj(C_QB, W_B), DH_B, gains[2:3])):
    qb_ref[t] = _rope_tile(p, tabb_ref, ROT_B // 2) * (DH_B ** -0.5 * LOG2_E)
  for t, p in enumerate(normed_tiles(proj(C_KB, W_B), DH_B, gains[3:4])):
    kb_ref[t] = _rope_tile(p, tabb_ref, ROT_B // 2)
  for t, p in enumerate(tiles(proj(C_VB, W_B))):
    vb_ref[t] = p
  gb_ref[...] = _silu(proj(C_GB, W_B)).astype(BF16)

  for t, p in enumerate(tiles(proj(C_QC, WQ_C))):
    qc_ref[:, LANES * t:LANES * (t + 1)] = _rope_tile(p, tabc_ref, DK_C // 2).astype(BF16)
  for t, p in enumerate(tiles(proj(C_KC, WQ_C))):
    y = _rope_tile(p, tabc_ref, DK_C // 2) * (DK_C ** -0.5)
    kc_ref[LANES * t:LANES * (t + 1), :] = y.T.astype(BF16)
  vc_ref[...] = proj(C_VC, WV_C).astype(BF16)
  gc_ref[...] = _silu(proj(C_GC, WV_C)).astype(BF16)


def _inproj(x, ng, shift, scale, w_ab, w_c, gains, taba, tabb, tabc):
  B, S, D = x.shape
  tm = min(ROW_TILE, S)
  n_pair = W_B // LANES

  def rows(width, dtype=BF16):
    return (pl.BlockSpec((None, tm, width), lambda i, b: (b, i, 0)),
            jax.ShapeDtypeStruct((B, S, width), dtype))

  def slabs():
    return (pl.BlockSpec((None, n_pair, tm, LANES), lambda i, b: (b, 0, i, 0)),
            jax.ShapeDtypeStruct((B, n_pair, S, LANES), F32))

  def feature_major(width):
    return (pl.BlockSpec((None, width, tm), lambda i, b: (b, 0, i)),
            jax.ShapeDtypeStruct((B, width, S), BF16))

  outs = [feature_major(W_A), rows(W_A), feature_major(W_A), rows(W_A),
          slabs(), slabs(), slabs(), rows(W_B),
          rows(WQ_C), feature_major(WQ_C), rows(WV_C), rows(WV_C)]
  tab_spec = pl.BlockSpec((3, tm, LANES), lambda i, b: (0, i, 0))
  return pl.pallas_call(
      _inproj_kernel,
      grid=(S // tm, B),
      in_specs=[
          pl.BlockSpec((None, tm, D), lambda i, b: (b, i, 0)),
          pl.BlockSpec((1, D), lambda i, b: (0, 0)),
          pl.BlockSpec((None, 1, D), lambda i, b: (b, 0, 0)),
          pl.BlockSpec((None, 1, D), lambda i, b: (b, 0, 0)),
          pl.BlockSpec((D, C_QC), lambda i, b: (0, 0)),
          pl.BlockSpec((D, IN_COLS_P - C_QC), lambda i, b: (0, 0)),
          pl.BlockSpec((4, LANES), lambda i, b: (0, 0)),
          tab_spec, tab_spec, tab_spec,
      ],
      out_specs=[o[0] for o in outs],
      out_shape=[o[1] for o in outs],
      scratch_shapes=[pltpu.VMEM((tm, D), BF16)],
      compiler_params=_cparams(2),
      name="inproj",
  )(x, ng, shift, scale, w_ab, w_c, gains, taba, tabb, tabc)


def _attn_a_kernel(qt_ref, k_ref, vt_ref, g_ref, lam_ref, sub_ref, o_ref,
                   vx_ref, ot_ref, *, lam_init):
  S = k_ref.shape[0]
  nq = qt_ref.shape[1]
  tk = min(K_CHUNK_A, S)

  @pl.when(pl.program_id(1) == 0)
  def _():
    for h in range(H_A):
      vx_ref[h, 0:DV_A, :] = vt_ref[DV_A * h:DV_A * (h + 1), :]
      vx_ref[h, DV_A:VX_ROWS_A, :] = jnp.ones((VX_ROWS_A - DV_A, S), BF16)

  lp = lam_ref[...]
  lam = (jnp.exp(jnp.sum(lp[0:1] * lp[1:2], axis=-1, keepdims=True))
         - jnp.exp(jnp.sum(lp[2:3] * lp[3:4], axis=-1, keepdims=True)) + lam_init)
  row = lax.broadcasted_iota(jnp.int32, (W_A, 1), 0)

  qt = qt_ref[...]

  def masked_q(h, m):
    lo = h * DV_A + m * DK_A
    return jnp.where((row >= lo) & (row < lo + DK_A), qt, jnp.zeros_like(qt))

  def scores(unit):
    h, m, j = unit
    return jnp.dot(k_ref[j * tk:(j + 1) * tk, :], masked_q(h, m), preferred_element_type=F32)

  units = [(h, m, j) for h in range(H_A) for j in range(S // tk) for m in range(2)]
  pending = [scores(u) for u in units[:LOOKAHEAD_A]]
  run = {}
  for idx, (h, m, j) in enumerate(units):
    st = pending.pop(0)
    if idx + LOOKAHEAD_A < len(units):
      pending.append(scores(units[idx + LOOKAHEAD_A]))
    cm = jnp.max(st, axis=0, keepdims=True)
    vx = vx_ref[h, :, j * tk:(j + 1) * tk]
    m_new = cm if (h, m) not in run else jnp.maximum(run[h, m][0], cm)
    e = jnp.exp2(st - m_new).astype(BF16)
    pv = jnp.dot(vx, e, preferred_element_type=F32)
    if (h, m) not in run:
      run[h, m] = (m_new, pv)
    else:
      m_old, acc = run[h, m]
      run[h, m] = (m_new, jnp.exp2(m_old - m_new) * acc + pv)
    if m == 1 and j == S // tk - 1:
      maps = [run[h, mm][1] for mm in range(2)]
      maps = [acc[0:DV_A] / acc[DV_A:DV_A + 1] for acc in maps]
      ot_ref[DV_A * h:DV_A * (h + 1), :] = maps[0] - lam * maps[1]

  for t in range(W_A // LANES):
    cols = slice(LANES * t, LANES * (t + 1))
    o = ot_ref[cols, :].T
    y = o * lax.rsqrt(_group_sumsq(o, DV_A) * (1.0 / DV_A) + EPS) * sub_ref[:, cols]
    y = y * (1.0 - lam_init) * g_ref[:, cols].astype(F32)
    o_ref[:, cols] = y.astype(BF16)


def _attn_a(qat, ka, vat, ga, lam_params, sub, lam_init):
  B, S, _ = ka.shape
  tq = min(Q_TILE_A, S)
  return pl.pallas_call(
      functools.partial(_attn_a_kernel, lam_init=lam_init),
      grid=(B, S // tq),
      in_specs=[
          pl.BlockSpec((None, W_A, tq), lambda b, i: (b, 0, i)),
          pl.BlockSpec((None, S, W_A), lambda b, i: (b, 0, 0)),
          pl.BlockSpec((None, W_A, S), lambda b, i: (b, 0, 0)),
          pl.BlockSpec((None, tq, W_A), lambda b, i: (b, i, 0)),
          pl.BlockSpec((4, DK_A), lambda b, i: (0, 0)),
          pl.BlockSpec((1, W_A), lambda b, i: (0, 0)),
      ],
      out_specs=pl.BlockSpec((None, tq, W_A), lambda b, i: (b, i, 0)),
      out_shape=jax.ShapeDtypeStruct((B, S, W_A), BF16),
      scratch_shapes=[
          pltpu.VMEM((H_A, VX_ROWS_A, S), BF16),
          pltpu.VMEM((W_A, tq), F32),
      ],
      compiler_params=_cparams(2),
      name="attn_a",
  )(qat, ka, vat, ga, lam_params, sub)


def _attn_b_kernel(q_ref, k_ref, v_ref, g_ref, o_ref, m_ref, l_ref, acc_ref, bias_ref):
  S = q_ref.shape[0]
  lane = lax.broadcasted_iota(jnp.int32, (1, LANES), 1)
  first = lane < DH_B
  ones = jnp.ones((K_WIN_B, LANES), BF16)

  def band(n_q, off):
    qi = lax.broadcasted_iota(jnp.int32, (2 * n_q, K_WIN_B), 0) & (n_q - 1)
    kj = lax.broadcasted_iota(jnp.int32, (2 * n_q, K_WIN_B), 1)
    return jnp.where(jnp.abs(kj - qi + off) <= N_SIDE, 0.0, NEG).astype(F32)

  @pl.when((pl.program_id(0) == 0) & (pl.program_id(1) == 0))
  def _():
    for idx, off in enumerate(WINDOW_OFFSETS_B):
      bias_ref[idx, 0:2 * Q_BLOCK_B, :] = band(Q_BLOCK_B, off)
      bias_ref[idx, 2 * Q_BLOCK_B:, :] = jnp.zeros((2 * K_WIN_B - 2 * Q_BLOCK_B, K_WIN_B), F32)
    bias_ref[len(WINDOW_OFFSETS_B)] = band(K_WIN_B, 0)

  for gi, d in enumerate(sorted(DILATIONS, reverse=True)):
    class_len = S // d
    qblk = class_len if class_len <= K_WIN_B else Q_BLOCK_B
    n_qblk = class_len // qblk
    n_items = min(ITEMS_B * Q_BLOCK_B // qblk, d * n_qblk)
    top, bot = slice(0, qblk), slice(qblk, 2 * qblk)
    shape = (qblk, LANES)

    def items(ib, carry, d=d, gi=gi, class_len=class_len, qblk=qblk, n_qblk=n_qblk, n_items=n_items,
              top=top, bot=bot, shape=shape):
      loaded = []
      for u in range(n_items):
        i = ib * n_items + u
        r = i // n_qblk
        a0 = (i % n_qblk) * qblk
        ak = jnp.clip(a0 - N_SIDE, 0, class_len - K_WIN_B)
        if d == 1:
          q_rows = pl.ds(pl.multiple_of(a0, qblk), qblk)
          k_rows = pl.ds(pl.multiple_of(ak, N_SIDE), K_WIN_B)
        else:
          q_rows = pl.ds(d * a0 + r, qblk, stride=d)
          k_rows = pl.ds(d * ak + r, K_WIN_B, stride=d)
        qt = q_ref[q_rows, :]
        q2 = jnp.concatenate([jnp.where(first, qt, 0.0), jnp.where(first, 0.0, qt)], axis=0).astype(BF16)
        kt = k_ref[k_rows, :].astype(BF16)
        vt = jnp.concatenate([v_ref[k_rows, :].astype(BF16), ones], axis=1)
        if qblk == K_WIN_B:
          which = len(WINDOW_OFFSETS_B)
        else:
          which = jnp.where(a0 == 0, 1, jnp.where(a0 == class_len - qblk, 2, 0))
        loaded.append((q_rows, which, q2, kt, vt))
      scores = [lax.dot_general(q2, kt, _NT, preferred_element_type=F32) for _, _, q2, kt, _ in loaded]
      probs = []
      for (_, which, _, _, _), s in zip(loaded, scores):
        s = s + bias_ref[which, 0:2 * qblk, :]
        m = jnp.max(s, axis=-1, keepdims=True)
        probs.append((m, jnp.exp2(s - m).astype(BF16)))
      outs = [jnp.dot(e, vt, preferred_element_type=F32) for (_, e), (_, _, _, _, vt) in zip(probs, loaded)]
      for (q_rows, _, _, _, _), (m, _), pv in zip(loaded, probs, outs):
        m_new = jnp.where(first, jnp.broadcast_to(m[top], shape), jnp.broadcast_to(m[bot], shape))
        l_new = jnp.where(first, pv[top, LANES:], pv[bot, LANES:])
        pv_new = jnp.where(first, pv[top, :LANES], pv[bot, :LANES])
        if gi == 0:
          m_ref[q_rows, :] = m_new
          l_ref[q_rows, :] = l_new
          acc_ref[q_rows, :] = pv_new
        else:
          m_old = m_ref[q_rows, :]
          m_tot = jnp.maximum(m_old, m_new)
          w_old = jnp.exp2(m_old - m_tot)
          w_new = jnp.exp2(m_new - m_tot)
          m_ref[q_rows, :] = m_tot
          l_ref[q_rows, :] = w_old * l_ref[q_rows, :] + w_new * l_new
          acc_ref[q_rows, :] = w_old * acc_ref[q_rows, :] + w_new * pv_new
      return carry

    lax.fori_loop(0, d * n_qblk // n_items, items, 0)

  def finish(c, carry):
    rows = pl.ds(pl.multiple_of(c * ROW_TILE, ROW_TILE), ROW_TILE)
    y = acc_ref[rows, :] / l_ref[rows, :] * g_ref[rows, :].astype(F32)
    o_ref[rows, :] = y.astype(BF16)
    return carry

  lax.fori_loop(0, S // ROW_TILE, finish, 0)


def _attn_b(qb, kb, vb, gb):
  B, n_pair, S, _ = qb.shape
  slab = pl.BlockSpec((None, None, S, LANES), lambda b, j: (b, j, 0, 0))
  cols = pl.BlockSpec((None, S, LANES), lambda b, j: (b, 0, j))
  return pl.pallas_call(
      _attn_b_kernel,
      grid=(B, n_pair),
      in_specs=[slab, slab, slab, cols],
      out_specs=cols,
      out_shape=jax.ShapeDtypeStruct((B, S, W_B), BF16),
      scratch_shapes=[pltpu.VMEM((S, LANES), F32)] * 3
      + [pltpu.VMEM((len(WINDOW_OFFSETS_B) + 1, 2 * K_WIN_B, K_WIN_B), F32)],
      compiler_params=_cparams(2),
      name="attn_b",
  )(qb, kb, vb, gb)


def _log_sigmoid(t):
  return jnp.minimum(t, 0.0) - jnp.log1p(jnp.exp(-jnp.abs(t)))


def _ret_kernel(q_ref, kt_ref, v_ref, g_ref, rdq_ref, rdh_ref, gn_ref, o_ref, u_ref, d_ref, tab_ref):
  S = q_ref.shape[0]
  C = RET_CHUNK
  N = S // C
  lgq = _log_sigmoid(rdq_ref[...])
  lgh = _log_sigmoid(rdh_ref[...])
  pos = lax.broadcasted_iota(jnp.int32, (C, 1), 0).astype(F32)
  tab_ref[0] = jnp.exp(lgq[0:1] * (pos + 1.0))
  tab_ref[1] = jnp.exp(lgq[1:2] * (C - pos))
  pos_row = lax.broadcasted_iota(jnp.int32, (1, C), 1).astype(F32)
  kdec = jnp.concatenate([jnp.exp(lgh[0:H_C] * (C - 1.0 - pos_row)), jnp.exp(lgh[H_C:] * pos_row)], axis=0)
  dd = (lax.broadcasted_iota(jnp.int32, (C, C), 0) - lax.broadcasted_iota(jnp.int32, (C, C), 1))
  dist = jnp.abs(dd).astype(F32)
  for h in range(H_C):
    d_ref[h] = jnp.exp(jnp.where(dd >= 0, lgh[h:h + 1], lgh[H_C + h:H_C + h + 1]) * dist)
  cdec = jnp.exp(lgh[:, 0:LANES] * float(C))

  lane = lax.broadcasted_iota(jnp.int32, (1, LANES), 1)
  first = lane < DKP_C

  def stack(fwd, bwd, hh):
    if hh == 0:
      return jnp.where(first, fwd, pltpu.roll(bwd, DKP_C, 1))
    return jnp.where(first, pltpu.roll(fwd, DKP_C, 1), bwd)

  def chunk_rows(n):
    return pl.ds(pl.multiple_of(n * C, C), C)

  def local_state(n, carry):
    rows = chunk_rows(n)
    stacks = []
    for h in range(H_C):
      kt = kt_ref[DKP_C * h:DKP_C * (h + 1), rows].astype(F32)
      stacks.append(jnp.concatenate([kt * kdec[h:h + 1], kt * kdec[H_C + h:H_C + h + 1]],
                                    axis=0).astype(BF16))
    states = [jnp.dot(st, v_ref[rows, DVP_C * h:DVP_C * (h + 1)], preferred_element_type=F32)
              for h, st in enumerate(stacks)]
    for h, u in enumerate(states):
      u_ref[n, h] = u
    return carry

  lax.fori_loop(0, N, local_state, 0)

  zero = tuple(jnp.zeros((DKP_C, DVP_C), F32) for _ in range(H_C))

  def scan_fwd(n, state):
    new = []
    for h in range(H_C):
      u = u_ref[n, h, 0:DKP_C, :]
      u_ref[n, h, 0:DKP_C, :] = state[h]
      new.append(state[h] * cdec[h:h + 1] + u)
    return tuple(new)

  lax.fori_loop(0, N, scan_fwd, zero)

  def scan_bwd(i, state):
    n = N - 1 - i
    new = []
    for h in range(H_C):
      u = u_ref[n, h, DKP_C:2 * DKP_C, :]
      u_ref[n, h, DKP_C:2 * DKP_C, :] = state[h]
      new.append(state[h] * cdec[H_C + h:H_C + h + 1] + u)
    return tuple(new)

  lax.fori_loop(0, N, scan_bwd, zero)

  def outputs(n, carry):
    rows = chunk_rows(n)
    qms, kts, qss = [], [], []
    for t in range(WQ_C // LANES):
      cols = slice(LANES * t, LANES * (t + 1))
      qt = q_ref[rows, cols]
      q32 = qt.astype(F32)
      qf = q32 * tab_ref[0, :, cols]
      qb = q32 * tab_ref[1, :, cols]
      for hh in range(2):
        keep = first if hh == 0 else jnp.logical_not(first)
        qms.append(jnp.where(keep, qt, jnp.zeros_like(qt)))
        kts.append(kt_ref[cols, rows])
        qss.append(stack(qf, qb, hh).astype(BF16))
    vcols = [slice(DVP_C * h, DVP_C * (h + 1)) for h in range(H_C)]
    atts = [jnp.dot(qms[h], kts[h], preferred_element_type=F32) for h in range(H_C)]
    cross = [jnp.dot(qss[h], u_ref[n, h].astype(BF16), preferred_element_type=F32) for h in range(H_C)]
    atts = [(atts[h] * d_ref[h]).astype(BF16) for h in range(H_C)]
    inner = [jnp.dot(atts[h], v_ref[rows, vcols[h]], preferred_element_type=F32) for h in range(H_C)]
    for h in range(H_C):
      o = inner[h] + cross[h]
      ms = jnp.sum(o * o, axis=-1, keepdims=True) * (1.0 / DV_C)
      y = o * lax.rsqrt(ms + EPS) * gn_ref[...]
      o_ref[rows, vcols[h]] = (y * g_ref[rows, vcols[h]].astype(F32)).astype(BF16)
    return carry

  lax.fori_loop(0, N, outputs, 0)


def _retention(qc, kct, vc, gc, rdq, rdh, gn):
  B, S, _ = qc.shape
  C = RET_CHUNK
  return pl.pallas_call(
      _ret_kernel,
      grid=(B,),
      in_specs=[
          pl.BlockSpec((None, S, WQ_C), lambda b: (b, 0, 0)),
          pl.BlockSpec((None, WQ_C, S), lambda b: (b, 0, 0)),
          pl.BlockSpec((None, S, WV_C), lambda b: (b, 0, 0)),
          pl.BlockSpec((None, S, WV_C), lambda b: (b, 0, 0)),
          pl.BlockSpec((2, WQ_C), lambda b: (0, 0)),
          pl.BlockSpec((2 * H_C, C), lambda b: (0, 0)),
          pl.BlockSpec((1, DVP_C), lambda b: (0, 0)),
      ],
      out_specs=pl.BlockSpec((None, S, WV_C), lambda b: (b, 0, 0)),
      out_shape=jax.ShapeDtypeStruct((B, S, WV_C), BF16),
      scratch_shapes=[
          pltpu.VMEM((S // C, H_C, 2 * DKP_C, DVP_C), F32),
          pltpu.VMEM((H_C, C, C), F32),
          pltpu.VMEM((2, C, WQ_C), F32),
      ],
      compiler_params=_cparams(1),
      name="retention",
  )(qc, kct, vc, gc, rdq, rdh, gn)


def _outproj_kernel(oa_ref, ob_ref, oc_ref, w_ref, x_ref, gate_ref, o_ref):
  y = jnp.dot(oa_ref[...], w_ref[0:W_A, :], preferred_element_type=F32)
  y += jnp.dot(ob_ref[...], w_ref[W_A:W_A + W_B, :], preferred_element_type=F32)
  y += jnp.dot(oc_ref[...], w_ref[W_A + W_B:OUT_ROWS_P, :], preferred_element_type=F32)
  o_ref[...] = x_ref[...] + gate_ref[...] * y


def _outproj(oa, ob, oc, w, x, gate):
  B, S, D = x.shape
  tm = min(OUT_ROW_TILE, S)

  def rows(width):
    return pl.BlockSpec((None, tm, width), lambda b, i: (b, i, 0))

  return pl.pallas_call(
      _outproj_kernel,
      grid=(B, S // tm),
      in_specs=[rows(W_A), rows(W_B), rows(WV_C),
                pl.BlockSpec((OUT_ROWS_P, D), lambda b, i: (0, 0)),
                rows(D),
                pl.BlockSpec((None, 1, D), lambda b, i: (b, 0, 0))],
      out_specs=rows(D),
      out_shape=jax.ShapeDtypeStruct((B, S, D), F32),
      compiler_params=_cparams(2),
      name="outproj",
  )(oa, ob, oc, w, x, gate)


def _rope_tables(S, theta, rot_dim, group):
  half = rot_dim // 2
  pos = jnp.arange(S, dtype=F32)
  inv = theta ** (-jnp.arange(0, rot_dim, 2, dtype=F32) / rot_dim)
  ang = pos[:, None] * inv[None, :]
  cos, sin = jnp.cos(ang), jnp.sin(ang)
  g = np.arange(LANES) % group
  idx = g % half
  in_first = jnp.asarray(g < half)[None, :]
  in_second = jnp.asarray((g >= half) & (g < rot_dim))[None, :]
  c = jnp.where(in_first | in_second, cos[:, idx], 1.0)
  s1 = jnp.where(in_first, -sin[:, idx], 0.0)
  s2 = jnp.where(in_second, sin[:, idx], 0.0)
  return jnp.stack([c, s1, s2]).astype(F32)


def _pad_heads(w, n_heads, d, d_pad, axis):
  shape = w.shape[:axis] + (n_heads, d) + w.shape[axis + 1:]
  pad = [(0, 0)] * (len(shape))
  pad[axis + 1] = (0, d_pad - d)
  out = jnp.pad(w.reshape(shape), pad)
  return out.reshape(w.shape[:axis] + (n_heads * d_pad,) + w.shape[axis + 1:])


def _prep_w_in(w_in):
  offs = np.concatenate([[0], np.cumsum(SPLIT_SIZES)])
  ax = w_in.ndim - 1
  w_ab = w_in[..., :offs[8]].astype(BF16)
  w_c = w_in[..., offs[8]:].astype(BF16)
  rel = offs[8:] - offs[8]
  sec = [w_c[..., rel[i]:rel[i + 1]] for i in range(4)]
  sec = [_pad_heads(sec[0], H_C, DK_C, DKP_C, ax), _pad_heads(sec[1], H_C, DK_C, DKP_C, ax),
         _pad_heads(sec[2], H_C, DV_C, DVP_C, ax), _pad_heads(sec[3], H_C, DV_C, DVP_C, ax)]
  return w_ab, jnp.concatenate(sec, axis=ax)


def _prep_w_out(w_out):
  w_out = w_out.astype(BF16)
  top = w_out[:, :W_A + W_B]
  bot = _pad_heads(w_out[:, W_A + W_B:], H_C, DV_C, DVP_C, 1)
  return jnp.concatenate([top, bot], axis=1)


@jax.jit
def _forward(x, c, norm_g, w_ada, b_ada, w_in, w_out, qn_a, kn_a, lambda_q1, lambda_k1,
             lambda_q2, lambda_k2, subln_a, qn_b, kn_b, ret_decay, gn_c):
  B, S, D = x.shape
  L = w_in.shape[0]
  rows = 8 * ((B + 7) // 8)
  c_pad = jnp.pad(c, ((0, rows - B), (0, 0)))
  mod = _adaln(c_pad, w_ada, b_ada)[:, :B]
  w_ab, w_c = _prep_w_in(w_in)
  w_out_p = _prep_w_out(w_out)
  taba = _rope_tables(S, ROT_THETA, ROT_A, DK_A)
  tabb = _rope_tables(S, ROT_THETA, ROT_B, DH_B)
  tabc = _rope_tables(S, RET_THETA, DK_C, DKP_C)
  gn_p = jnp.pad(gn_c, ((0, 0), (0, DVP_C - DV_C)))
  for l in range(L):
    shift = mod[l, :, None, 0:D]
    scale = mod[l, :, None, D:2 * D]
    gate = mod[l, :, None, 2 * D:3 * D]
    gains = jnp.stack([jnp.tile(qn_a[l], LANES // DK_A), jnp.tile(kn_a[l], LANES // DK_A),
                       jnp.tile(qn_b[l], LANES // DH_B), jnp.tile(kn_b[l], LANES // DH_B)])
    (qa, ka, va, ga, qb, kb, vb, gb, qc, kc, vc, gc) = _inproj(
        x, norm_g[l][None, :], shift, scale, w_ab[l], w_c[l], gains, taba, tabb, tabc)
    lam_init = 0.8 - 0.6 * math.exp(-0.3 * l)
    lam_params = jnp.stack([lambda_q1[l], lambda_k1[l], lambda_q2[l], lambda_k2[l]])
    oa = _attn_a(qa, ka, va, ga, lam_params, jnp.tile(subln_a[l], H_A)[None, :], lam_init)
    ob = _attn_b(qb, kb, vb, gb)
    rdq = jnp.repeat(ret_decay[l], DKP_C, axis=1)
    rdh = jnp.broadcast_to(ret_decay[l].reshape(2 * H_C, 1), (2 * H_C, RET_CHUNK))
    oc = _retention(qc, kc, vc, gc, rdq, rdh, gn_p[l][None, :])
    x = _outproj(oa, ob, oc, w_out_p[l], x, gate)
  return x


def kernel(x, c, norm_g, w_ada, b_ada, w_in, w_out, qn_a, kn_a, lambda_q1, lambda_k1, lambda_q2,
           lambda_k2, subln_a, qn_b, kn_b, ret_decay, gn_c):
  return _forward(x, c, norm_g, w_ada, b_ada, w_in, w_out, qn_a, kn_a, lambda_q1, lambda_k1,
                  lambda_q2, lambda_k2, subln_a, qn_b, kn_b, ret_decay, gn_c)
```
